```python
import math
import jax, jax.numpy as jnp
from jax import lax
import numpy as np

D_MODEL = 1024
BATCH = 4
SEQ = 8192
DEPTH = 1

HEAD_DIM = 128
MOBA_HEADS = 6
MOBA_BLOCK = 256
MOBA_TOPK = 3
MOBA_QCHUNK = 32
GMLP_GROUPS = 6
GMLP_GROUP_DIM = 128
GMLP_CHUNK = 128
XATTN_HEADS = 4
MEM_LEN = 256
ROPE_THETA = 500000.0
ROT_DIM = HEAD_DIM // 4
EPS = 1e-6

A_WIDTH = MOBA_HEADS * HEAD_DIM
B_WIDTH = GMLP_GROUPS * GMLP_GROUP_DIM
C_WIDTH = XATTN_HEADS * HEAD_DIM
IN_SPLITS = (A_WIDTH, A_WIDTH, A_WIDTH, A_WIDTH, B_WIDTH, B_WIDTH, B_WIDTH, C_WIDTH, C_WIDTH, D_MODEL, D_MODEL, D_MODEL)
IN_WIDTH = 4 * A_WIDTH + 3 * B_WIDTH + 2 * C_WIDTH + 3 * D_MODEL

kernel_name = "hybrid_moba_gmlp_xattn_gated_layer"


def rms_norm(x, g):
    xf = x.astype(jnp.float32)
    y = xf * lax.rsqrt(jnp.mean(xf * xf, axis=-1, keepdims=True) + EPS)
    return (y * g.astype(jnp.float32)).astype(x.dtype)


def layer_norm(x, g):
    xf = x.astype(jnp.float32)
    mu = jnp.mean(xf, axis=-1, keepdims=True)
    var = jnp.mean(jnp.square(xf - mu), axis=-1, keepdims=True)
    return ((xf - mu) * lax.rsqrt(var + EPS) * g.astype(jnp.float32)).astype(x.dtype)


def split_cols(t, widths):
    offs, acc = [], 0
    for w in widths[:-1]:
        acc += w
        offs.append(acc)
    return jnp.split(t, offs, axis=-1)


def partial_rope(x, pos):
    half = ROT_DIM // 2
    inv_freq = ROPE_THETA ** (-jnp.arange(0, ROT_DIM, 2, dtype=jnp.float32) / ROT_DIM)
    ang = pos.astype(jnp.float32)[:, None] * inv_freq[None, :]
    cos = jnp.cos(ang)[None, :, None, :].astype(x.dtype)
    sin = jnp.sin(ang)[None, :, None, :].astype(x.dtype)
    x1, x2, rest = x[..., :half], x[..., half:ROT_DIM], x[..., ROT_DIM:]
    return jnp.concatenate([x1 * cos - x2 * sin, x2 * cos + x1 * sin, rest], axis=-1)


def moba_attention(q, k, v):
    B, S, H, dh = q.shape
    n_blocks = -(-S // MOBA_BLOCK)
    k_sel_n = min(MOBA_TOPK, n_blocks)
    pad = n_blocks * MOBA_BLOCK - S
    scale = HEAD_DIM ** -0.5
    qh = jnp.transpose(q, (0, 2, 1, 3))
    kh = jnp.pad(jnp.transpose(k, (0, 2, 1, 3)), ((0, 0), (0, 0), (0, pad), (0, 0)))
    vh = jnp.pad(jnp.transpose(v, (0, 2, 1, 3)), ((0, 0), (0, 0), (0, pad), (0, 0)))
    k_blk = kh.reshape(B, H, n_blocks, MOBA_BLOCK, dh)
    v_blk = vh.reshape(B, H, n_blocks, MOBA_BLOCK, dh)
    k_mean = jnp.mean(k_blk.astype(jnp.float32), axis=3)
    b_ix = jnp.arange(B)[:, None, None, None]
    h_ix = jnp.arange(H)[None, :, None, None]
    blk_ids = jnp.arange(n_blocks)

    def chunk_fn(c):
        start = c * MOBA_QCHUNK
        qc = lax.dynamic_slice_in_dim(qh, start, MOBA_QCHUNK, axis=2)
        pos_q = start + jnp.arange(MOBA_QCHUNK)
        own = start // MOBA_BLOCK
        bs = jnp.einsum('bhqd,bhnd->bhqn', qc.astype(jnp.float32), k_mean)
        bs = jnp.where((blk_ids < own)[None, None, None, :], bs, -jnp.inf)
        _, sel_idx = lax.top_k(bs, k_sel_n)
        sel_valid = sel_idx < own
        k_sel = k_blk[b_ix, h_ix, sel_idx]
        v_sel = v_blk[b_ix, h_ix, sel_idx]
        lg_sel = jnp.einsum('bhqd,bhqkpd->bhqkp', qc, k_sel).astype(jnp.float32) * scale
        lg_sel = jnp.where(sel_valid[..., None], lg_sel, -jnp.inf)
        lg_sel = lg_sel.reshape(B, H, MOBA_QCHUNK, k_sel_n * MOBA_BLOCK)
        k_own = lax.dynamic_index_in_dim(k_blk, own, axis=2, keepdims=False)
        v_own = lax.dynamic_index_in_dim(v_blk, own, axis=2, keepdims=False)
        key_pos = own * MOBA_BLOCK + jnp.arange(MOBA_BLOCK)
        causal = key_pos[None, :] <= pos_q[:, None]
        lg_own = jnp.einsum('bhqd,bhpd->bhqp', qc, k_own).astype(jnp.float32) * scale
        lg_own = jnp.where(causal[None, None], lg_own, -jnp.inf)
        p = jax.nn.softmax(jnp.concatenate([lg_sel, lg_own], axis=-1), axis=-1).astype(v.dtype)
        p_sel = p[..., :k_sel_n * MOBA_BLOCK].reshape(B, H, MOBA_QCHUNK, k_sel_n, MOBA_BLOCK)
        p_own = p[..., k_sel_n * MOBA_BLOCK:]
        return (jnp.einsum('bhqkp,bhqkpd->bhqd', p_sel, v_sel)
                + jnp.einsum('bhqp,bhpd->bhqd', p_own, v_own))

    out = lax.map(chunk_fn, jnp.arange(S // MOBA_QCHUNK))
    return jnp.transpose(out, (1, 0, 3, 2, 4)).reshape(B, S, H * dh)


def gmlp_spatial_gate(u, v, ln_g, w_spatial, b_spatial):
    B, S, _ = v.shape
    vn = layer_norm(v, ln_g).reshape(B, S // GMLP_CHUNK, GMLP_CHUNK, GMLP_GROUPS, GMLP_GROUP_DIM)
    w_causal = jnp.tril(w_spatial)
    mixed = jnp.einsum('gts,bnsgd->bntgd', w_causal, vn) + jnp.transpose(b_spatial)[None, None, :, :, None]
    return u * mixed.reshape(B, S, B_WIDTH)


def memory_cross_attention(q, mem_n, w_mem_kv):
    B, S, _ = q.shape
    M = mem_n.shape[1]
    mk, mv = jnp.split(mem_n @ w_mem_kv, 2, axis=-1)
    qh = q.reshape(B, S, XATTN_HEADS, HEAD_DIM)
    kh = mk.reshape(B, M, XATTN_HEADS, HEAD_DIM)
    vh = mv.reshape(B, M, XATTN_HEADS, HEAD_DIM)
    lg = jnp.einsum('bshd,bmhd->bhsm', qh, kh).astype(jnp.float32) * (HEAD_DIM ** -0.5)
    p = jax.nn.softmax(lg, axis=-1).astype(q.dtype)
    return jnp.einsum('bhsm,bmhd->bshd', p, vh).reshape(B, S, C_WIDTH)


def setup_inputs(seed: int = 0) -> dict:
    key = jax.random.key(seed)
    ks = jax.random.split(key, 16)
    f32 = jnp.float32

    def nrm(k, shape, fan_in):
        return jax.random.normal(k, shape, f32) * (fan_in ** -0.5)

    def gain(k, shape):
        return 1.0 + 0.1 * jax.random.normal(k, shape, f32)

    return {
        "x": jax.random.normal(ks[0], (BATCH, SEQ, D_MODEL), f32),
        "mem": jax.random.normal(ks[1], (BATCH, MEM_LEN, D_MODEL), f32),
        "norm_g": gain(ks[2], (DEPTH, D_MODEL)),
        "mem_norm_g": gain(ks[3], (DEPTH, D_MODEL)),
        "final_norm_g": gain(ks[4], (D_MODEL,)),
        "w_in": nrm(ks[5], (DEPTH, D_MODEL, IN_WIDTH), D_MODEL),
        "w_mem_kv": nrm(ks[6], (DEPTH, D_MODEL, 2 * C_WIDTH), D_MODEL),
        "gmlp_ln_g": gain(ks[7], (DEPTH, B_WIDTH)),
        "w_spatial": nrm(ks[8], (DEPTH, GMLP_GROUPS, GMLP_CHUNK, GMLP_CHUNK), GMLP_CHUNK),
        "b_spatial": gain(ks[9], (DEPTH, GMLP_GROUPS, GMLP_CHUNK)),
        "w_branch_a": nrm(ks[10], (DEPTH, A_WIDTH, D_MODEL), A_WIDTH),
        "w_branch_b": nrm(ks[11], (DEPTH, B_WIDTH, D_MODEL), B_WIDTH),
        "w_branch_c": nrm(ks[12], (DEPTH, C_WIDTH, D_MODEL), C_WIDTH),
        "w_out": nrm(ks[13], (DEPTH, D_MODEL, D_MODEL), D_MODEL),
    }


def reference(x, mem, norm_g, mem_norm_g, final_norm_g, w_in, w_mem_kv, gmlp_ln_g,
              w_spatial, b_spatial, w_branch_a, w_branch_b, w_branch_c, w_out):
    B, S, _ = x.shape
    pos = jnp.arange(S)
    for layer in range(DEPTH):
        h = rms_norm(x, norm_g[layer])
        (qa, ka, va, ga, ub, vb, gb, qc, gc, ma, mb, mc) = split_cols(h @ w_in[layer], IN_SPLITS)
        qa = partial_rope(qa.reshape(B, S, MOBA_HEADS, HEAD_DIM), pos)
        ka = partial_rope(ka.reshape(B, S, MOBA_HEADS, HEAD_DIM), pos)
        va = va.reshape(B, S, MOBA_HEADS, HEAD_DIM)
        out_a = moba_attention(qa, ka, va) * jax.nn.silu(ga)
        out_b = gmlp_spatial_gate(ub, vb, gmlp_ln_g[layer], w_spatial[layer], b_spatial[layer]) * jax.nn.silu(gb)
        mem_n = rms_norm(mem, mem_norm_g[layer])
        out_c = memory_cross_attention(qc, mem_n, w_mem_kv[layer]) * jax.nn.silu(gc)
        y = (jax.nn.sigmoid(ma) * (out_a @ w_branch_a[layer])
             + jax.nn.sigmoid(mb) * (out_b @ w_branch_b[layer])
             + jax.nn.sigmoid(mc) * (out_c @ w_branch_c[layer]))
        x = x + y @ w_out[layer]
    return rms_norm(x, final_norm_g)
```

```python
import functools
import math

import jax
import jax.numpy as jnp
from jax import lax
from jax.experimental import pallas as pl
from jax.experimental.pallas import tpu as pltpu

HEAD_DIM = 128
MOBA_HEADS = 6
MOBA_BLOCK = 256
MOBA_TOPK = 3
GMLP_GROUPS = 6
GMLP_CHUNK = 128
XATTN_HEADS = 4
ROPE_THETA = 500000.0
ROT_DIM = HEAD_DIM // 4
EPS = 1e-6

A_WIDTH = MOBA_HEADS * HEAD_DIM
B_WIDTH = GMLP_GROUPS * HEAD_DIM
C_WIDTH = XATTN_HEADS * HEAD_DIM

V7X_VMEM_LIMIT_BYTES = 58 * 1024 * 1024

PROJ_ROWS = 512
GMLP_ROWS = 2 * GMLP_CHUNK

_BF16 = jnp.bfloat16
_F32 = jnp.float32
_NEG_INF = float("-inf")


def _rms_norm(x, g):
    return x * lax.rsqrt(jnp.mean(x * x, axis=-1, keepdims=True) + EPS) * g


def _silu(x):
    return x * jax.nn.sigmoid(x)


def _dot(a, b):
    return jnp.dot(a, b, preferred_element_type=_F32)


def _dot_nt(a, b):
    return lax.dot_general(a, b, (((1,), (1,)), ((), ())), preferred_element_type=_F32)


def _mem_kv_kernel(mem_ref, g_ref, w_ref, mk_ref, mv_ref):
    mn = _rms_norm(mem_ref[...], g_ref[...]).astype(_BF16)
    kv = _dot(mn, w_ref[...])
    mk_ref[...] = kv[:, :C_WIDTH].astype(_BF16)
    mv_ref[...] = kv[:, C_WIDTH:].astype(_BF16)


def _proj_kernel(x_ref, g_ref, w_ref, cos_ref, sin_ref, lng_ref, wsp_ref, bsp_ref,
                 mk_ref, mv_ref, wb_ref, wc_ref,
                 q_ref, k_ref, v_ref, sga_ref, kmean_ref, ybc_ref, sma_ref,
                 outb_scr, outc_scr):
    rows = x_ref.shape[0]
    d_model = x_ref.shape[1]
    h = _rms_norm(x_ref[...], g_ref[...]).astype(_BF16)

    off = [0]

    def proj(width):
        a = off[0]
        off[0] = a + width
        return _dot(h, w_ref[:, a:a + width])

    cos = cos_ref[...]
    sin = sin_ref[...]
    lane = lax.broadcasted_iota(jnp.int32, (rows, HEAD_DIM), 1)
    first_half = lane < (ROT_DIM // 2)

    def rope(t):
        swapped = jnp.where(first_half,
                            pltpu.roll(t, HEAD_DIM - ROT_DIM // 2, 1),
                            pltpu.roll(t, ROT_DIM // 2, 1))
        return t * cos + swapped * sin

    qa = proj(A_WIDTH)
    for hd in range(MOBA_HEADS):
        sl = slice(hd * HEAD_DIM, (hd + 1) * HEAD_DIM)
        q_ref[:, sl] = rope(qa[:, sl]).astype(_BF16)
    ka = proj(A_WIDTH)
    for hd in range(MOBA_HEADS):
        sl = slice(hd * HEAD_DIM, (hd + 1) * HEAD_DIM)
        kh = rope(ka[:, sl])
        k_ref[:, sl] = kh.astype(_BF16)
        kmean_ref[:, sl] = jnp.mean(
            kh.reshape(rows // MOBA_BLOCK, MOBA_BLOCK, HEAD_DIM), axis=1)
    v_ref[...] = proj(A_WIDTH).astype(_BF16)
    sga_ref[...] = _silu(proj(A_WIDTH)).astype(_BF16)

    ub = proj(B_WIDTH)
    vb = proj(B_WIDTH)
    gb = proj(B_WIDTH)
    mu = jnp.mean(vb, axis=-1, keepdims=True)
    var = jnp.mean(jnp.square(vb - mu), axis=-1, keepdims=True)
    vn = ((vb - mu) * lax.rsqrt(var + EPS) * lng_ref[...]).astype(_BF16)
    gate_b = ub * _silu(gb)
    r_i = lax.broadcasted_iota(jnp.int32, (GMLP_ROWS, GMLP_ROWS), 0)
    c_i = lax.broadcasted_iota(jnp.int32, (GMLP_ROWS, GMLP_ROWS), 1)
    causal = r_i >= c_i
    bsp = bsp_ref[...]
    for g in range(GMLP_GROUPS):
        sl = slice(g * HEAD_DIM, (g + 1) * HEAD_DIM)
        wg = jnp.where(causal, wsp_ref[g], 0.0).astype(_BF16)
        bias = bsp[:, g:g + 1]
        for r in range(rows // GMLP_ROWS):
            rs = slice(r * GMLP_ROWS, (r + 1) * GMLP_ROWS)
            mixed = _dot(wg, vn[rs, sl]) + bias
            outb_scr[rs, sl] = (gate_b[rs, sl] * mixed).astype(_BF16)

    qc = proj(C_WIDTH).astype(_BF16)
    gc = proj(C_WIDTH)
    scale = HEAD_DIM ** -0.5
    for hd in range(XATTN_HEADS):
        sl = slice(hd * HEAD_DIM, (hd + 1) * HEAD_DIM)
        s = _dot_nt(qc[:, sl], mk_ref[:, sl]) * scale
        m = jnp.max(s, axis=-1, keepdims=True)
        p = jnp.exp(s - m)
        l = jnp.sum(p, axis=-1, keepdims=True)
        o = _dot(p.astype(_BF16), mv_ref[:, sl]) / l
        outc_scr[:, sl] = (o * _silu(gc[:, sl])).astype(_BF16)

    sma_ref[...] = jax.nn.sigmoid(proj(d_model)).astype(_BF16)
    yb = jax.nn.sigmoid(proj(d_model)) * _dot(outb_scr[...], wb_ref[...])
    yc = jax.nn.sigmoid(proj(d_model)) * _dot(outc_scr[...], wc_ref[...])
    ybc_ref[...] = (yb + yc).astype(_BF16)


def _moba_out_kernel(q_ref, k_ref, v_ref, kmean_ref, sga_ref, ybc_ref, sma_ref, x_ref,
                     wa_ref, wo_ref, fg_ref, o_ref, outa_scr):
    own = pl.program_id(1)
    n_blocks = kmean_ref.shape[0]
    tq = q_ref.shape[0]
    scale = HEAD_DIM ** -0.5
    blk = lax.broadcasted_iota(jnp.int32, (tq, n_blocks), 1)
    valid = blk < own
    row = lax.broadcasted_iota(jnp.int32, (tq, MOBA_BLOCK), 0)
    col = lax.broadcasted_iota(jnp.int32, (tq, MOBA_BLOCK), 1)
    causal = col <= row
    own_start = pl.multiple_of(own * MOBA_BLOCK, MOBA_BLOCK)

    for hd in range(MOBA_HEADS):
        sl = slice(hd * HEAD_DIM, (hd + 1) * HEAD_DIM)
        qh = q_ref[:, sl]

        gs = _dot_nt(qh, kmean_ref[:, sl].astype(_BF16))
        gs = jnp.where(valid, gs, _NEG_INF)
        sel = jnp.zeros((tq, n_blocks), dtype=jnp.bool_)
        for _ in range(MOBA_TOPK):
            top = jnp.max(gs, axis=-1, keepdims=True)
            idx = jnp.min(jnp.where(gs == top, blk, n_blocks), axis=-1, keepdims=True)
            pick = blk == idx
            sel = jnp.logical_or(sel, pick)
            gs = jnp.where(pick, _NEG_INF, gs)
        bias = jnp.where(jnp.logical_and(sel, valid), 0.0, _NEG_INF)

        s = _dot_nt(qh, k_ref[pl.ds(own_start, MOBA_BLOCK), sl]) * scale
        s = jnp.where(causal, s, _NEG_INF)
        m0 = jnp.max(s, axis=-1, keepdims=True)
        p = jnp.exp(s - m0)
        l0 = jnp.sum(p, axis=-1, keepdims=True)
        acc0 = _dot(p.astype(_BF16), v_ref[pl.ds(own_start, MOBA_BLOCK), sl])

        def body(j, carry, qh=qh, bias=bias, sl=sl):
            m, l, acc = carry
            start = pl.multiple_of(j * MOBA_BLOCK, MOBA_BLOCK)
            s = _dot_nt(qh, k_ref[pl.ds(start, MOBA_BLOCK), sl]) * scale
            s = s + jnp.sum(jnp.where(blk == j, bias, 0.0), axis=-1, keepdims=True)
            m_new = jnp.maximum(m, jnp.max(s, axis=-1, keepdims=True))
            alpha = jnp.exp(m - m_new)
            p = jnp.exp(s - m_new)
            l = alpha * l + jnp.sum(p, axis=-1, keepdims=True)
            acc = alpha * acc + _dot(p.astype(_BF16), v_ref[pl.ds(start, MOBA_BLOCK), sl])
            return m_new, l, acc

        _, l, acc = lax.fori_loop(0, own, body, (m0, l0, acc0))
        outa_scr[:, sl] = (acc / l * sga_ref[:, sl].astype(_F32)).astype(_BF16)

    ya = _dot(outa_scr[...], wa_ref[...])
    y = sma_ref[...].astype(_F32) * ya + ybc_ref[...].astype(_F32)
    z = x_ref[...] + _dot(y.astype(_BF16), wo_ref[...])
    o_ref[...] = _rms_norm(z, fg_ref[...])


def _rope_tables(seq):
    half = ROT_DIM // 2
    inv_freq = ROPE_THETA ** (-jnp.arange(0, ROT_DIM, 2, dtype=_F32) / ROT_DIM)
    ang = jnp.arange(seq).astype(_F32)[:, None] * inv_freq[None, :]
    cos, sin = jnp.cos(ang), jnp.sin(ang)
    rest = HEAD_DIM - ROT_DIM
    cos_t = jnp.concatenate([cos, cos, jnp.ones((seq, rest), _F32)], axis=1)
    sin_t = jnp.concatenate([-sin, sin, jnp.zeros((seq, rest), _F32)], axis=1)
    return cos_t, sin_t


def _const_spec(shape):
    nd = len(shape)
    return pl.BlockSpec(shape, lambda *_: (0,) * nd, pipeline_mode=pl.Buffered(1))


def kernel(x, mem, norm_g, mem_norm_g, final_norm_g, w_in, w_mem_kv, gmlp_ln_g,
           w_spatial, b_spatial, w_branch_a, w_branch_b, w_branch_c, w_out):
    batch, seq, d_model = x.shape
    mem_len = mem.shape[1]
    assert w_in.shape[0] == 1, "single layer"
    assert seq % PROJ_ROWS == 0 and PROJ_ROWS % MOBA_BLOCK == 0 and PROJ_ROWS % GMLP_ROWS == 0
    n_tok = batch * seq
    n_blocks = seq // MOBA_BLOCK
    in_width = w_in.shape[-1]

    x2 = x.reshape(n_tok, d_model)
    w_in_b = w_in[0].astype(_BF16)
    cos_t, sin_t = _rope_tables(seq)
    eye = jnp.eye(GMLP_ROWS // GMLP_CHUNK, dtype=_F32)
    wsp_bd = jnp.einsum("ab,gts->gatbs", eye, w_spatial[0]).reshape(
        GMLP_GROUPS, GMLP_ROWS, GMLP_ROWS)
    bsp_t = jnp.tile(jnp.transpose(b_spatial[0]), (GMLP_ROWS // GMLP_CHUNK, 1))

    params = functools.partial(pltpu.CompilerParams, vmem_limit_bytes=V7X_VMEM_LIMIT_BYTES)

    mk, mv = pl.pallas_call(
        _mem_kv_kernel,
        grid=(batch,),
        in_specs=[
            pl.BlockSpec((None, mem_len, d_model), lambda b: (b, 0, 0)),
            pl.BlockSpec((1, d_model), lambda b: (0, 0)),
            pl.BlockSpec((d_model, 2 * C_WIDTH), lambda b: (0, 0)),
        ],
        out_specs=[
            pl.BlockSpec((None, mem_len, C_WIDTH), lambda b: (b, 0, 0)),
            pl.BlockSpec((None, mem_len, C_WIDTH), lambda b: (b, 0, 0)),
        ],
        out_shape=[jax.ShapeDtypeStruct((batch, mem_len, C_WIDTH), _BF16)] * 2,
        compiler_params=params(dimension_semantics=("arbitrary",)),
        name="mem_kv",
    )(mem, mem_norm_g[0].reshape(1, d_model), w_mem_kv[0].astype(_BF16))

    tiles_per_seq = seq // PROJ_ROWS
    blocks_per_tile = PROJ_ROWS // MOBA_BLOCK
    row_spec = lambda w: pl.BlockSpec((PROJ_ROWS, w), lambda i: (i, 0))
    pos_spec = pl.BlockSpec((PROJ_ROWS, HEAD_DIM), lambda i: (i % tiles_per_seq, 0))
    mem_spec = pl.BlockSpec((None, mem_len, C_WIDTH), lambda i: (i // tiles_per_seq, 0, 0))
    qa, ka, va, sga, kmean, ybc, sma = pl.pallas_call(
        _proj_kernel,
        grid=(n_tok // PROJ_ROWS,),
        in_specs=[
            row_spec(d_model),
            _const_spec((1, d_model)),
            _const_spec((d_model, in_width)),
            pos_spec, pos_spec,
            _const_spec((1, B_WIDTH)),
            _const_spec((GMLP_GROUPS, GMLP_ROWS, GMLP_ROWS)),
            _const_spec((GMLP_ROWS, GMLP_GROUPS)),
            mem_spec, mem_spec,
            _const_spec((B_WIDTH, d_model)),
            _const_spec((C_WIDTH, d_model)),
        ],
        out_specs=[
            row_spec(A_WIDTH), row_spec(A_WIDTH), row_spec(A_WIDTH), row_spec(A_WIDTH),
            pl.BlockSpec((None, blocks_per_tile, A_WIDTH), lambda i: (i, 0, 0)),
            row_spec(d_model), row_spec(d_model),
        ],
        out_shape=[
            jax.ShapeDtypeStruct((n_tok, A_WIDTH), _BF16),
            jax.ShapeDtypeStruct((n_tok, A_WIDTH), _BF16),
            jax.ShapeDtypeStruct((n_tok, A_WIDTH), _BF16),
            jax.ShapeDtypeStruct((n_tok, A_WIDTH), _BF16),
            jax.ShapeDtypeStruct((n_tok // PROJ_ROWS, blocks_per_tile, A_WIDTH), _F32),
            jax.ShapeDtypeStruct((n_tok, d_model), _BF16),
            jax.ShapeDtypeStruct((n_tok, d_model), _BF16),
        ],
        scratch_shapes=[
            pltpu.VMEM((PROJ_ROWS, B_WIDTH), _BF16),
            pltpu.VMEM((PROJ_ROWS, C_WIDTH), _BF16),
        ],
        compiler_params=params(dimension_semantics=("arbitrary",)),
        name="proj_branches",
    )(x2, norm_g[0].reshape(1, d_model), w_in_b, cos_t, sin_t,
      gmlp_ln_g[0].reshape(1, B_WIDTH), wsp_bd, bsp_t, mk, mv,
      w_branch_b[0].astype(_BF16), w_branch_c[0].astype(_BF16))

    kmean = kmean.reshape(batch, n_blocks, A_WIDTH)
    k3 = ka.reshape(batch, seq, A_WIDTH)
    v3 = va.reshape(batch, seq, A_WIDTH)
    tile_spec = lambda w: pl.BlockSpec((MOBA_BLOCK, w), lambda b, i: (b * n_blocks + i, 0))
    seq_spec = pl.BlockSpec((None, seq, A_WIDTH), lambda b, i: (b, 0, 0),
                            pipeline_mode=pl.Buffered(1))
    out = pl.pallas_call(
        _moba_out_kernel,
        grid=(batch, n_blocks),
        in_specs=[
            tile_spec(A_WIDTH),
            seq_spec, seq_spec,
            pl.BlockSpec((None, n_blocks, A_WIDTH), lambda b, i: (b, 0, 0)),
            tile_spec(A_WIDTH), tile_spec(d_model), tile_spec(d_model), tile_spec(d_model),
            _const_spec((A_WIDTH, d_model)),
            _const_spec((d_model, d_model)),
            _const_spec((1, d_model)),
        ],
        out_specs=tile_spec(d_model),
        out_shape=jax.ShapeDtypeStruct((n_tok, d_model), _F32),
        scratch_shapes=[pltpu.VMEM((MOBA_BLOCK, A_WIDTH), _BF16)],
        compiler_params=params(dimension_semantics=("arbitrary", "arbitrary")),
        name="moba_merge_out",
    )(qa, k3, v3, kmean, sga, ybc, sma, x2,
      w_branch_a[0].astype(_BF16), w_out[0].astype(_BF16), final_norm_g.reshape(1, d_model))
    return out.reshape(batch, seq, d_model)
```

```python
import functools
import math

import jax
import jax.numpy as jnp
from jax import lax
from jax.experimental import pallas as pl
from jax.experimental.pallas import tpu as pltpu

HEAD_DIM = 128
MOBA_HEADS = 6
MOBA_BLOCK = 256
MOBA_TOPK = 3
GMLP_GROUPS = 6
GMLP_CHUNK = 128
XATTN_HEADS = 4
ROPE_THETA = 500000.0
ROT_DIM = HEAD_DIM // 4
EPS = 1e-6

A_WIDTH = MOBA_HEADS * HEAD_DIM
B_WIDTH = GMLP_GROUPS * HEAD_DIM
C_WIDTH = XATTN_HEADS * HEAD_DIM

V7X_VMEM_LIMIT_BYTES = 58 * 1024 * 1024

PROJ_ROWS = 512
GMLP_ROWS = 2 * GMLP_CHUNK

QK_LOG2_SCALE = HEAD_DIM ** -0.5 * math.log2(math.e)
MASK_BIAS = -1e30
AUG_DIM = 2 * HEAD_DIM

_BF16 = jnp.bfloat16
_F32 = jnp.float32
_NEG_INF = float("-inf")


def _rms_norm(x, g):
    return x * lax.rsqrt(jnp.mean(x * x, axis=-1, keepdims=True) + EPS) * g


def _silu(x):
    return x * jax.nn.sigmoid(x)


def _dot(a, b):
    return jnp.dot(a, b, preferred_element_type=_F32)


def _dot_nt(a, b):
    return lax.dot_general(a, b, (((1,), (1,)), ((), ())), preferred_element_type=_F32)


def _mem_kv_kernel(mem_ref, g_ref, w_ref, mk_ref, mv_ref):
    mn = _rms_norm(mem_ref[...], g_ref[...]).astype(_BF16)
    kv = _dot(mn, w_ref[...])
    mk_ref[...] = kv[:, :C_WIDTH].astype(_BF16)
    mv_ref[...] = kv[:, C_WIDTH:].astype(_BF16)


def _proj_kernel(x_ref, g_ref, w_ref, cos_ref, sin_ref, lng_ref, wsp_ref, bsp_ref,
                 mk_ref, mv_ref, wb_ref, wc_ref,
                 qt_ref, k_ref, vt_ref, sga_ref, kmean_ref, ybc_ref, sma_ref,
                 outb_scr, outc_scr):
    rows = x_ref.shape[0]
    d_model = x_ref.shape[1]
    h = _rms_norm(x_ref[...], g_ref[...]).astype(_BF16)

    off = [0]

    def proj(width):
        a = off[0]
        off[0] = a + width
        return _dot(h, w_ref[:, a:a + width])

    cos = cos_ref[...]
    sin = sin_ref[...]
    lane = lax.broadcasted_iota(jnp.int32, (rows, HEAD_DIM), 1)
    first_half = lane < (ROT_DIM // 2)

    def rope(t):
        swapped = jnp.where(first_half,
                            pltpu.roll(t, HEAD_DIM - ROT_DIM // 2, 1),
                            pltpu.roll(t, ROT_DIM // 2, 1))
        return t * cos + swapped * sin

    qa = proj(A_WIDTH)
    for hd in range(MOBA_HEADS):
        sl = slice(hd * HEAD_DIM, (hd + 1) * HEAD_DIM)
        qh = rope(qa[:, sl]) * QK_LOG2_SCALE
        for r in range(rows // MOBA_BLOCK):
            rs = slice(r * MOBA_BLOCK, (r + 1) * MOBA_BLOCK)
            qt_ref[r, sl, :] = qh[rs].T.astype(_BF16)
    ka = proj(A_WIDTH)
    for hd in range(MOBA_HEADS):
        sl = slice(hd * HEAD_DIM, (hd + 1) * HEAD_DIM)
        kh = rope(ka[:, sl])
        k_ref[:, sl] = kh.astype(_BF16)
        kmean_ref[:, sl] = jnp.mean(
            kh.reshape(rows // MOBA_BLOCK, MOBA_BLOCK, HEAD_DIM), axis=1)
    va = proj(A_WIDTH)
    for r in range(rows // MOBA_BLOCK):
        rs = slice(r * MOBA_BLOCK, (r + 1) * MOBA_BLOCK)
        vt_ref[r] = va[rs].T.astype(_BF16)
    sga_ref[...] = _silu(proj(A_WIDTH)).astype(_BF16)

    ub = proj(B_WIDTH)
    vb = proj(B_WIDTH)
    gb = proj(B_WIDTH)
    mu = jnp.mean(vb, axis=-1, keepdims=True)
    var = jnp.mean(jnp.square(vb - mu), axis=-1, keepdims=True)
    vn = ((vb - mu) * lax.rsqrt(var + EPS) * lng_ref[...]).astype(_BF16)
    gate_b = ub * _silu(gb)
    r_i = lax.broadcasted_iota(jnp.int32, (GMLP_ROWS, GMLP_ROWS), 0)
    c_i = lax.broadcasted_iota(jnp.int32, (GMLP_ROWS, GMLP_ROWS), 1)
    causal = r_i >= c_i
    bsp = bsp_ref[...]
    for g in range(GMLP_GROUPS):
        sl = slice(g * HEAD_DIM, (g + 1) * HEAD_DIM)
        wg = jnp.where(causal, wsp_ref[g], 0.0).astype(_BF16)
        bias = bsp[:, g:g + 1]
        for r in range(rows // GMLP_ROWS):
            rs = slice(r * GMLP_ROWS, (r + 1) * GMLP_ROWS)
            mixed = _dot(wg, vn[rs, sl]) + bias
            outb_scr[rs, sl] = (gate_b[rs, sl] * mixed).astype(_BF16)

    qc = proj(C_WIDTH).astype(_BF16)
    gc = proj(C_WIDTH)
    scale = HEAD_DIM ** -0.5
    for hd in range(XATTN_HEADS):
        sl = slice(hd * HEAD_DIM, (hd + 1) * HEAD_DIM)
        s = _dot_nt(qc[:, sl], mk_ref[:, sl]) * scale
        m = jnp.max(s, axis=-1, keepdims=True)
        p = jnp.exp(s - m)
        l = jnp.sum(p, axis=-1, keepdims=True)
        o = _dot(p.astype(_BF16), mv_ref[:, sl]) / l
        outc_scr[:, sl] = (o * _silu(gc[:, sl])).astype(_BF16)

    sma_ref[...] = jax.nn.sigmoid(proj(d_model)).astype(_BF16)
    yb = jax.nn.sigmoid(proj(d_model)) * _dot(outb_scr[...], wb_ref[...])
    yc = jax.nn.sigmoid(proj(d_model)) * _dot(outc_scr[...], wc_ref[...])
    ybc_ref[...] = (yb + yc).astype(_BF16)


def _moba_out_kernel(qt_ref, k_ref, vt_ref, kmean_ref, sga_ref, ybc_ref, sma_ref, x_ref,
                     wa_ref, wo_ref, fg_ref, o_ref,
                     qaug_scr, sa_scr, sb_scr, m_scr, l_scr, acc_scr, outa_scr):
    own = pl.program_id(1)
    n_blocks = kmean_ref.shape[0]
    tq = qt_ref.shape[1]
    blk = lax.broadcasted_iota(jnp.int32, (n_blocks, tq), 0)
    valid = blk < own
    key_i = lax.broadcasted_iota(jnp.int32, (MOBA_BLOCK, tq), 0)
    qry_i = lax.broadcasted_iota(jnp.int32, (MOBA_BLOCK, tq), 1)
    causal = key_i <= qry_i
    own_start = pl.multiple_of(own * MOBA_BLOCK, MOBA_BLOCK)

    for hd in range(MOBA_HEADS):
        sl = slice(hd * HEAD_DIM, (hd + 1) * HEAD_DIM)
        qt = qt_ref[sl, :]

        gs = _dot(kmean_ref[:, sl].astype(_BF16), qt)
        gs = jnp.where(valid, gs, _NEG_INF)
        sel = jnp.zeros((n_blocks, tq), dtype=jnp.bool_)
        for _ in range(MOBA_TOPK):
            top = jnp.max(gs, axis=0, keepdims=True)
            idx = jnp.min(jnp.where(gs == top, blk, n_blocks), axis=0, keepdims=True)
            pick = blk == idx
            sel = jnp.logical_or(sel, pick)
            gs = jnp.where(pick, _NEG_INF, gs)
        bias = jnp.where(jnp.logical_and(sel, valid), 0.0, MASK_BIAS)
        qaug_scr[hd, 0:HEAD_DIM, :] = qt
        qaug_scr[hd, HEAD_DIM:HEAD_DIM + n_blocks, :] = bias.astype(_BF16)
        qaug_scr[hd, HEAD_DIM + n_blocks:AUG_DIM, :] = jnp.zeros(
            (AUG_DIM - HEAD_DIM - n_blocks, tq), _BF16)

        s = _dot(k_ref[pl.ds(own_start, MOBA_BLOCK), sl], qt)
        s = jnp.where(causal, s, _NEG_INF)
        m0 = jnp.max(s, axis=0, keepdims=True)
        p = jnp.exp2(s - m0)
        m_scr[hd] = m0
        l_scr[hd] = jnp.sum(p, axis=0, keepdims=True)
        acc_scr[hd] = _dot(vt_ref[own, sl, :], p.astype(_BF16))

    lane = lax.broadcasted_iota(jnp.int32, (MOBA_BLOCK, HEAD_DIM), 1)

    def masked_scores(j, hd):
        sl = slice(hd * HEAD_DIM, (hd + 1) * HEAD_DIM)
        start = pl.multiple_of(j * MOBA_BLOCK, MOBA_BLOCK)
        onehot = jnp.where(lane == j, 1.0, 0.0).astype(_BF16)
        kaug = jnp.concatenate([k_ref[pl.ds(start, MOBA_BLOCK), sl], onehot], axis=1)
        return _dot(kaug, qaug_scr[hd])

    def softmax_pv(j, hd, s):
        sl = slice(hd * HEAD_DIM, (hd + 1) * HEAD_DIM)
        m_old = m_scr[hd]
        m_new = jnp.maximum(m_old, jnp.max(s, axis=0, keepdims=True))
        alpha = jnp.exp2(m_old - m_new)
        p = jnp.exp2(s - m_new)
        m_scr[hd] = m_new
        l_scr[hd] = alpha * l_scr[hd] + jnp.sum(p, axis=0, keepdims=True)
        acc_scr[hd] = alpha * acc_scr[hd] + _dot(vt_ref[j, sl, :], p.astype(_BF16))

    for hd in range(MOBA_HEADS):
        sa_scr[hd] = masked_scores(0, hd)

    def pair_body(jj, carry):
        j0 = 2 * jj
        for hd in range(MOBA_HEADS):
            sb_scr[hd] = masked_scores(j0 + 1, hd)
            softmax_pv(j0, hd, sa_scr[hd])
        for hd in range(MOBA_HEADS):
            sa_scr[hd] = masked_scores(j0 + 2, hd)
            softmax_pv(j0 + 1, hd, sb_scr[hd])
        return carry

    lax.fori_loop(0, own // 2, pair_body, 0)

    @pl.when(own % 2 == 1)
    def _():
        for hd in range(MOBA_HEADS):
            softmax_pv(own - 1, hd, sa_scr[hd])

    for hd in range(MOBA_HEADS):
        sl = slice(hd * HEAD_DIM, (hd + 1) * HEAD_DIM)
        o = (acc_scr[hd] / l_scr[hd]).T
        outa_scr[:, sl] = (o * sga_ref[:, sl].astype(_F32)).astype(_BF16)

    ya = _dot(outa_scr[...], wa_ref[...])
    y = sma_ref[...].astype(_F32) * ya + ybc_ref[...].astype(_F32)
    z = x_ref[...] + _dot(y.astype(_BF16), wo_ref[...])
    o_ref[...] = _rms_norm(z, fg_ref[...])


def _rope_tables(seq):
    inv_freq = ROPE_THETA ** (-jnp.arange(0, ROT_DIM, 2, dtype=_F32) / ROT_DIM)
    ang = jnp.arange(seq).astype(_F32)[:, None] * inv_freq[None, :]
    cos, sin = jnp.cos(ang), jnp.sin(ang)
    rest = HEAD_DIM - ROT_DIM
    cos_t = jnp.concatenate([cos, cos, jnp.ones((seq, rest), _F32)], axis=1)
    sin_t = jnp.concatenate([-sin, sin, jnp.zeros((seq, rest), _F32)], axis=1)
    return cos_t, sin_t


def _const_spec(shape):
    nd = len(shape)
    return pl.BlockSpec(shape, lambda *_: (0,) * nd, pipeline_mode=pl.Buffered(1))


def kernel(x, mem, norm_g, mem_norm_g, final_norm_g, w_in, w_mem_kv, gmlp_ln_g,
           w_spatial, b_spatial, w_branch_a, w_branch_b, w_branch_c, w_out):
    batch, seq, d_model = x.shape
    mem_len = mem.shape[1]
    assert w_in.shape[0] == 1, "single layer"
    assert seq % PROJ_ROWS == 0 and PROJ_ROWS % MOBA_BLOCK == 0 and PROJ_ROWS % GMLP_ROWS == 0
    n_tok = batch * seq
    n_blocks = seq // MOBA_BLOCK
    assert HEAD_DIM + n_blocks <= AUG_DIM and n_blocks % 16 == 0
    in_width = w_in.shape[-1]

    x2 = x.reshape(n_tok, d_model)
    w_in_b = w_in[0].astype(_BF16)
    cos_t, sin_t = _rope_tables(seq)
    eye = jnp.eye(GMLP_ROWS // GMLP_CHUNK, dtype=_F32)
    wsp_bd = jnp.einsum("ab,gts->gatbs", eye, w_spatial[0]).reshape(
        GMLP_GROUPS, GMLP_ROWS, GMLP_ROWS)
    bsp_t = jnp.tile(jnp.transpose(b_spatial[0]), (GMLP_ROWS // GMLP_CHUNK, 1))

    params = functools.partial(pltpu.CompilerParams, vmem_limit_bytes=V7X_VMEM_LIMIT_BYTES)

    mk, mv = pl.pallas_call(
        _mem_kv_kernel,
        grid=(batch,),
        in_specs=[
            pl.BlockSpec((None, mem_len, d_model), lambda b: (b, 0, 0)),
            pl.BlockSpec((1, d_model), lambda b: (0, 0)),
            pl.BlockSpec((d_model, 2 * C_WIDTH), lambda b: (0, 0)),
        ],
        out_specs=[
            pl.BlockSpec((None, mem_len, C_WIDTH), lambda b: (b, 0, 0)),
            pl.BlockSpec((None, mem_len, C_WIDTH), lambda b: (b, 0, 0)),
        ],
        out_shape=[jax.ShapeDtypeStruct((batch, mem_len, C_WIDTH), _BF16)] * 2,
        compiler_params=params(dimension_semantics=("arbitrary",)),
        name="mem_kv",
    )(mem, mem_norm_g[0].reshape(1, d_model), w_mem_kv[0].astype(_BF16))

    tiles_per_seq = seq // PROJ_ROWS
    blocks_per_tile = PROJ_ROWS // MOBA_BLOCK
    row_spec = lambda w: pl.BlockSpec((PROJ_ROWS, w), lambda i: (i, 0))
    blk_t_spec = pl.BlockSpec((blocks_per_tile, A_WIDTH, MOBA_BLOCK), lambda i: (i, 0, 0))
    pos_spec = pl.BlockSpec((PROJ_ROWS, HEAD_DIM), lambda i: (i % tiles_per_seq, 0))
    mem_spec = pl.BlockSpec((None, mem_len, C_WIDTH), lambda i: (i // tiles_per_seq, 0, 0))
    blk_t_shape = jax.ShapeDtypeStruct((n_tok // MOBA_BLOCK, A_WIDTH, MOBA_BLOCK), _BF16)
    qt, ka, vt, sga, kmean, ybc, sma = pl.pallas_call(
        _proj_kernel,
        grid=(n_tok // PROJ_ROWS,),
        in_specs=[
            row_spec(d_model),
            _const_spec((1, d_model)),
            _const_spec((d_model, in_width)),
            pos_spec, pos_spec,
            _const_spec((1, B_WIDTH)),
            _const_spec((GMLP_GROUPS, GMLP_ROWS, GMLP_ROWS)),
            _const_spec((GMLP_ROWS, GMLP_GROUPS)),
            mem_spec, mem_spec,
            _const_spec((B_WIDTH, d_model)),
            _const_spec((C_WIDTH, d_model)),
        ],
        out_specs=[
            blk_t_spec, row_spec(A_WIDTH), blk_t_spec, row_spec(A_WIDTH),
            pl.BlockSpec((None, blocks_per_tile, A_WIDTH), lambda i: (i, 0, 0)),
            row_spec(d_model), row_spec(d_model),
        ],
        out_shape=[
            blk_t_shape,
            jax.ShapeDtypeStruct((n_tok, A_WIDTH), _BF16),
            blk_t_shape,
            jax.ShapeDtypeStruct((n_tok, A_WIDTH), _BF16),
            jax.ShapeDtypeStruct((n_tok // PROJ_ROWS, blocks_per_tile, A_WIDTH), _F32),
            jax.ShapeDtypeStruct((n_tok, d_model), _BF16),
            jax.ShapeDtypeStruct((n_tok, d_model), _BF16),
        ],
        scratch_shapes=[
            pltpu.VMEM((PROJ_ROWS, B_WIDTH), _BF16),
            pltpu.VMEM((PROJ_ROWS, C_WIDTH), _BF16),
        ],
        compiler_params=params(dimension_semantics=("arbitrary",)),
        name="proj_branches",
    )(x2, norm_g[0].reshape(1, d_model), w_in_b, cos_t, sin_t,
      gmlp_ln_g[0].reshape(1, B_WIDTH), wsp_bd, bsp_t, mk, mv,
      w_branch_b[0].astype(_BF16), w_branch_c[0].astype(_BF16))

    kmean = kmean.reshape(batch, n_blocks, A_WIDTH)
    k3 = ka.reshape(batch, seq, A_WIDTH)
    vt4 = vt.reshape(batch, n_blocks, A_WIDTH, MOBA_BLOCK)
    tile_spec = lambda w: pl.BlockSpec((MOBA_BLOCK, w), lambda b, i: (b * n_blocks + i, 0))
    out = pl.pallas_call(
        _moba_out_kernel,
        grid=(batch, n_blocks),
        in_specs=[
            pl.BlockSpec((None, A_WIDTH, MOBA_BLOCK), lambda b, i: (b * n_blocks + i, 0, 0)),
            pl.BlockSpec((None, seq, A_WIDTH), lambda b, i: (b, 0, 0),
                         pipeline_mode=pl.Buffered(1)),
            pl.BlockSpec((None, n_blocks, A_WIDTH, MOBA_BLOCK), lambda b, i: (b, 0, 0, 0),
                         pipeline_mode=pl.Buffered(1)),
            pl.BlockSpec((None, n_blocks, A_WIDTH), lambda b, i: (b, 0, 0)),
            tile_spec(A_WIDTH), tile_spec(d_model), tile_spec(d_model), tile_spec(d_model),
            _const_spec((A_WIDTH, d_model)),
            _const_spec((d_model, d_model)),
            _const_spec((1, d_model)),
        ],
        out_specs=tile_spec(d_model),
        out_shape=jax.ShapeDtypeStruct((n_tok, d_model), _F32),
        scratch_shapes=[
            pltpu.VMEM((MOBA_HEADS, AUG_DIM, MOBA_BLOCK), _BF16),
            pltpu.VMEM((MOBA_HEADS, MOBA_BLOCK, MOBA_BLOCK), _F32),
            pltpu.VMEM((MOBA_HEADS, MOBA_BLOCK, MOBA_BLOCK), _F32),
            pltpu.VMEM((MOBA_HEADS, 1, MOBA_BLOCK), _F32),
            pltpu.VMEM((MOBA_HEADS, 1, MOBA_BLOCK), _F32),
            pltpu.VMEM((MOBA_HEADS, HEAD_DIM, MOBA_BLOCK), _F32),
            pltpu.VMEM((MOBA_BLOCK, A_WIDTH), _BF16),
        ],
        compiler_params=params(dimension_semantics=("arbitrary", "arbitrary")),
        name="moba_merge_out",
    )(qt, k3, vt4, kmean, sga, ybc, sma, x2,
      w_branch_a[0].astype(_BF16), w_out[0].astype(_BF16), final_norm_g.reshape(1, d_model))
    return out.reshape(batch, seq, d_model)
```

```python
import functools
import math

import jax
import jax.numpy as jnp
from jax import lax
from jax.experimental import pallas as pl
from jax.experimental.pallas import tpu as pltpu

HEAD_DIM = 128
MOBA_HEADS = 6
MOBA_BLOCK = 256
MOBA_TOPK = 3
GMLP_GROUPS = 6
GMLP_CHUNK = 128
XATTN_HEADS = 4
ROPE_THETA = 500000.0
ROT_DIM = HEAD_DIM // 4
EPS = 1e-6

A_WIDTH = MOBA_HEADS * HEAD_DIM
B_WIDTH = GMLP_GROUPS * HEAD_DIM
C_WIDTH = XATTN_HEADS * HEAD_DIM

V7X_VMEM_LIMIT_BYTES = 58 * 1024 * 1024

PROJ_ROWS = 512
GMLP_ROWS = 2 * GMLP_CHUNK

QK_LOG2_SCALE = HEAD_DIM ** -0.5 * math.log2(math.e)
MASK_BIAS = -1e30
AUG_DIM = 2 * HEAD_DIM

_BF16 = jnp.bfloat16
_F32 = jnp.float32
_NEG_INF = float("-inf")


def _rms_norm(x, g):
    return x * lax.rsqrt(jnp.mean(x * x, axis=-1, keepdims=True) + EPS) * g


def _silu(x):
    return x * jax.nn.sigmoid(x)


def _dot(a, b):
    return jnp.dot(a, b, preferred_element_type=_F32)


def _dot_nt(a, b):
    return lax.dot_general(a, b, (((1,), (1,)), ((), ())), preferred_element_type=_F32)


def _mem_kv_kernel(mem_ref, g_ref, w_ref, mk_ref, mv_ref):
    mn = _rms_norm(mem_ref[...], g_ref[...]).astype(_BF16)
    kv = _dot(mn, w_ref[...])
    mk_ref[...] = kv[:, :C_WIDTH].astype(_BF16)
    mv_ref[...] = kv[:, C_WIDTH:].astype(_BF16)


def _proj_kernel(x_ref, g_ref, w_ref, cos_ref, sin_ref, lng_ref, wsp_ref, bsp_ref,
                 mk_ref, mv_ref, wb_ref, wc_ref,
                 qt_ref, k_ref, vt_ref, sga_ref, kmean_ref, ybc_ref, sma_ref,
                 outb_scr, outc_scr):
    rows = x_ref.shape[0]
    d_model = x_ref.shape[1]
    h = _rms_norm(x_ref[...], g_ref[...]).astype(_BF16)

    off = [0]

    def proj(width):
        a = off[0]
        off[0] = a + width
        return _dot(h, w_ref[:, a:a + width])

    cos = cos_ref[...]
    sin = sin_ref[...]
    lane = lax.broadcasted_iota(jnp.int32, (rows, HEAD_DIM), 1)
    first_half = lane < (ROT_DIM // 2)

    def rope(t):
        swapped = jnp.where(first_half,
                            pltpu.roll(t, HEAD_DIM - ROT_DIM // 2, 1),
                            pltpu.roll(t, ROT_DIM // 2, 1))
        return t * cos + swapped * sin

    qa = proj(A_WIDTH)
    for hd in range(MOBA_HEADS):
        sl = slice(hd * HEAD_DIM, (hd + 1) * HEAD_DIM)
        qh = rope(qa[:, sl]) * QK_LOG2_SCALE
        for r in range(rows // MOBA_BLOCK):
            rs = slice(r * MOBA_BLOCK, (r + 1) * MOBA_BLOCK)
            qt_ref[r, sl, :] = qh[rs].T.astype(_BF16)
    ka = proj(A_WIDTH)
    for hd in range(MOBA_HEADS):
        sl = slice(hd * HEAD_DIM, (hd + 1) * HEAD_DIM)
        kh = rope(ka[:, sl])
        k_ref[:, sl] = kh.astype(_BF16)
        kmean_ref[:, sl] = jnp.mean(
            kh.reshape(rows // MOBA_BLOCK, MOBA_BLOCK, HEAD_DIM), axis=1)
    va = proj(A_WIDTH)
    for r in range(rows // MOBA_BLOCK):
        rs = slice(r * MOBA_BLOCK, (r + 1) * MOBA_BLOCK)
        vt_ref[r] = va[rs].T.astype(_BF16)
    sga_ref[...] = _silu(proj(A_WIDTH)).astype(_BF16)

    ub = proj(B_WIDTH)
    vb = proj(B_WIDTH)
    gb = proj(B_WIDTH)
    mu = jnp.mean(vb, axis=-1, keepdims=True)
    var = jnp.mean(jnp.square(vb - mu), axis=-1, keepdims=True)
    vn = ((vb - mu) * lax.rsqrt(var + EPS) * lng_ref[...]).astype(_BF16)
    gate_b = ub * _silu(gb)
    r_i = lax.broadcasted_iota(jnp.int32, (GMLP_ROWS, GMLP_ROWS), 0)
    c_i = lax.broadcasted_iota(jnp.int32, (GMLP_ROWS, GMLP_ROWS), 1)
    causal = r_i >= c_i
    bsp = bsp_ref[...]
    for g in range(GMLP_GROUPS):
        sl = slice(g * HEAD_DIM, (g + 1) * HEAD_DIM)
        wg = jnp.where(causal, wsp_ref[g], 0.0).astype(_BF16)
        bias = bsp[:, g:g + 1]
        for r in range(rows // GMLP_ROWS):
            rs = slice(r * GMLP_ROWS, (r + 1) * GMLP_ROWS)
            mixed = _dot(wg, vn[rs, sl]) + bias
            outb_scr[rs, sl] = (gate_b[rs, sl] * mixed).astype(_BF16)

    qc = proj(C_WIDTH).astype(_BF16)
    gc = proj(C_WIDTH)
    scale = HEAD_DIM ** -0.5
    for hd in range(XATTN_HEADS):
        sl = slice(hd * HEAD_DIM, (hd + 1) * HEAD_DIM)
        s = _dot_nt(qc[:, sl], mk_ref[:, sl]) * scale
        m = jnp.max(s, axis=-1, keepdims=True)
        p = jnp.exp(s - m)
        l = jnp.sum(p, axis=-1, keepdims=True)
        o = _dot(p.astype(_BF16), mv_ref[:, sl]) / l
        outc_scr[:, sl] = (o * _silu(gc[:, sl])).astype(_BF16)

    sma_ref[...] = jax.nn.sigmoid(proj(d_model)).astype(_BF16)
    yb = jax.nn.sigmoid(proj(d_model)) * _dot(outb_scr[...], wb_ref[...])
    yc = jax.nn.sigmoid(proj(d_model)) * _dot(outc_scr[...], wc_ref[...])
    ybc_ref[...] = (yb + yc).astype(_BF16)


def _moba_out_kernel(qt_ref, k_ref, vt_ref, kmean_ref, sga_ref, ybc_ref, sma_ref, x_ref,
                     wa_ref, wo_ref, fg_ref, o_ref,
                     qaug_scr, sa_scr, sb_scr, sd_scr, m_scr, l_scr, acc_scr, outa_scr):
    own = pl.program_id(1)
    n_blocks = kmean_ref.shape[0]
    tq = qt_ref.shape[1]
    blk = lax.broadcasted_iota(jnp.int32, (n_blocks, tq), 0)
    valid = blk < own
    key_i = lax.broadcasted_iota(jnp.int32, (MOBA_BLOCK, HEAD_DIM), 0)
    qry_i = lax.broadcasted_iota(jnp.int32, (MOBA_BLOCK, HEAD_DIM), 1)
    own_start = pl.multiple_of(own * MOBA_BLOCK, MOBA_BLOCK)

    for hd in range(MOBA_HEADS):
        sl = slice(hd * HEAD_DIM, (hd + 1) * HEAD_DIM)
        qt = qt_ref[sl, :]

        gs = _dot(kmean_ref[:, sl].astype(_BF16), qt)
        gs = jnp.where(valid, gs, _NEG_INF)
        sel = jnp.zeros((n_blocks, tq), dtype=jnp.bool_)
        for _ in range(MOBA_TOPK):
            top = jnp.max(gs, axis=0, keepdims=True)
            idx = jnp.min(jnp.where(gs == top, blk, n_blocks), axis=0, keepdims=True)
            pick = blk == idx
            sel = jnp.logical_or(sel, pick)
            gs = jnp.where(pick, _NEG_INF, gs)
        bias = jnp.where(jnp.logical_and(sel, valid), 0.0, MASK_BIAS)
        qaug_scr[hd, 0:HEAD_DIM, :] = qt
        qaug_scr[hd, HEAD_DIM:HEAD_DIM + n_blocks, :] = bias.astype(_BF16)
        qaug_scr[hd, HEAD_DIM + n_blocks:AUG_DIM, :] = jnp.zeros(
            (AUG_DIM - HEAD_DIM - n_blocks, tq), _BF16)

        sd_scr[hd] = _dot(k_ref[pl.ds(own_start, MOBA_BLOCK), sl], qt)
        m_scr[hd] = jnp.full((1, tq), _NEG_INF, _F32)
        l_scr[hd] = jnp.zeros((1, tq), _F32)
        acc_scr[hd] = jnp.zeros((HEAD_DIM, tq), _F32)

    lane = lax.broadcasted_iota(jnp.int32, (MOBA_BLOCK, HEAD_DIM), 1)

    def masked_scores(j, hd):
        sl = slice(hd * HEAD_DIM, (hd + 1) * HEAD_DIM)
        start = pl.multiple_of(j * MOBA_BLOCK, MOBA_BLOCK)
        onehot = jnp.where(lane == j, 1.0, 0.0).astype(_BF16)
        kaug = jnp.concatenate([k_ref[pl.ds(start, MOBA_BLOCK), sl], onehot], axis=1)
        return _dot(kaug, qaug_scr[hd])

    def softmax_pv(j, hd, s_ref, causal=False):
        sl = slice(hd * HEAD_DIM, (hd + 1) * HEAD_DIM)
        ps, alphas = [], []
        for half in range(tq // HEAD_DIM):
            cs = slice(half * HEAD_DIM, (half + 1) * HEAD_DIM)
            s = s_ref[hd, :, cs]
            if causal:
                s = jnp.where(key_i <= qry_i + half * HEAD_DIM, s, _NEG_INF)
            m_old = m_scr[hd, :, cs]
            m_new = jnp.maximum(m_old, jnp.max(s, axis=0, keepdims=True))
            alpha = jnp.exp2(m_old - m_new)
            p = jnp.exp2(s - m_new)
            m_scr[hd, :, cs] = m_new
            l_scr[hd, :, cs] = alpha * l_scr[hd, :, cs] + jnp.sum(p, axis=0, keepdims=True)
            ps.append(p.astype(_BF16))
            alphas.append(alpha)
        pv = _dot(vt_ref[j, sl, :], jnp.concatenate(ps, axis=1))
        for half, alpha in enumerate(alphas):
            cs = slice(half * HEAD_DIM, (half + 1) * HEAD_DIM)
            acc_scr[hd, :, cs] = alpha * acc_scr[hd, :, cs] + pv[:, cs]

    for hd in range(MOBA_HEADS):
        sa_scr[hd] = masked_scores(0, hd)
        sb_scr[hd] = masked_scores(1, hd)

    def pair_body(jj, carry):
        j0 = 2 * jj
        for hd in range(MOBA_HEADS):
            s_next = masked_scores(j0 + 2, hd)
            softmax_pv(j0, hd, sa_scr)
            sa_scr[hd] = s_next
        j3 = jnp.minimum(j0 + 3, n_blocks - 1)
        for hd in range(MOBA_HEADS):
            s_next = masked_scores(j3, hd)
            softmax_pv(j0 + 1, hd, sb_scr)
            sb_scr[hd] = s_next
        return carry

    lax.fori_loop(0, own // 2, pair_body, 0)

    @pl.when(own % 2 == 1)
    def _():
        for hd in range(MOBA_HEADS):
            softmax_pv(own - 1, hd, sa_scr)

    for hd in range(MOBA_HEADS):
        softmax_pv(own, hd, sd_scr, causal=True)

    for hd in range(MOBA_HEADS):
        sl = slice(hd * HEAD_DIM, (hd + 1) * HEAD_DIM)
        o = (acc_scr[hd] / l_scr[hd]).T
        outa_scr[:, sl] = (o * sga_ref[:, sl].astype(_F32)).astype(_BF16)

    ya = _dot(outa_scr[...], wa_ref[...])
    y = sma_ref[...].astype(_F32) * ya + ybc_ref[...].astype(_F32)
    z = x_ref[...] + _dot(y.astype(_BF16), wo_ref[...])
    o_ref[...] = _rms_norm(z, fg_ref[...])


def _rope_tables(seq):
    inv_freq = ROPE_THETA ** (-jnp.arange(0, ROT_DIM, 2, dtype=_F32) / ROT_DIM)
    ang = jnp.arange(seq).astype(_F32)[:, None] * inv_freq[None, :]
    cos, sin = jnp.cos(ang), jnp.sin(ang)
    rest = HEAD_DIM - ROT_DIM
    cos_t = jnp.concatenate([cos, cos, jnp.ones((seq, rest), _F32)], axis=1)
    sin_t = jnp.concatenate([-sin, sin, jnp.zeros((seq, rest), _F32)], axis=1)
    return cos_t, sin_t


def _const_spec(shape):
    nd = len(shape)
    return pl.BlockSpec(shape, lambda *_: (0,) * nd, pipeline_mode=pl.Buffered(1))


def kernel(x, mem, norm_g, mem_norm_g, final_norm_g, w_in, w_mem_kv, gmlp_ln_g,
           w_spatial, b_spatial, w_branch_a, w_branch_b, w_branch_c, w_out):
    batch, seq, d_model = x.shape
    mem_len = mem.shape[1]
    assert w_in.shape[0] == 1, "single layer"
    assert seq % PROJ_ROWS == 0 and PROJ_ROWS % MOBA_BLOCK == 0 and PROJ_ROWS % GMLP_ROWS == 0
    n_tok = batch * seq
    n_blocks = seq // MOBA_BLOCK
    assert HEAD_DIM + n_blocks <= AUG_DIM and n_blocks % 16 == 0
    in_width = w_in.shape[-1]

    x2 = x.reshape(n_tok, d_model)
    w_in_b = w_in[0].astype(_BF16)
    cos_t, sin_t = _rope_tables(seq)
    eye = jnp.eye(GMLP_ROWS // GMLP_CHUNK, dtype=_F32)
    wsp_bd = jnp.einsum("ab,gts->gatbs", eye, w_spatial[0]).reshape(
        GMLP_GROUPS, GMLP_ROWS, GMLP_ROWS)
    bsp_t = jnp.tile(jnp.transpose(b_spatial[0]), (GMLP_ROWS // GMLP_CHUNK, 1))

    params = functools.partial(pltpu.CompilerParams, vmem_limit_bytes=V7X_VMEM_LIMIT_BYTES)

    mk, mv = pl.pallas_call(
        _mem_kv_kernel,
        grid=(batch,),
        in_specs=[
            pl.BlockSpec((None, mem_len, d_model), lambda b: (b, 0, 0)),
            pl.BlockSpec((1, d_model), lambda b: (0, 0)),
            pl.BlockSpec((d_model, 2 * C_WIDTH), lambda b: (0, 0)),
        ],
        out_specs=[
            pl.BlockSpec((None, mem_len, C_WIDTH), lambda b: (b, 0, 0)),
            pl.BlockSpec((None, mem_len, C_WIDTH), lambda b: (b, 0, 0)),
        ],
        out_shape=[jax.ShapeDtypeStruct((batch, mem_len, C_WIDTH), _BF16)] * 2,
        compiler_params=params(dimension_semantics=("arbitrary",)),
        name="mem_kv",
    )(mem, mem_norm_g[0].reshape(1, d_model), w_mem_kv[0].astype(_BF16))

    tiles_per_seq = seq // PROJ_ROWS
    blocks_per_tile = PROJ_ROWS // MOBA_BLOCK
    row_spec = lambda w: pl.BlockSpec((PROJ_ROWS, w), lambda i: (i, 0))
    blk_t_spec = pl.BlockSpec((blocks_per_tile, A_WIDTH, MOBA_BLOCK), lambda i: (i, 0, 0))
    pos_spec = pl.BlockSpec((PROJ_ROWS, HEAD_DIM), lambda i: (i % tiles_per_seq, 0))
    mem_spec = pl.BlockSpec((None, mem_len, C_WIDTH), lambda i: (i // tiles_per_seq, 0, 0))
    blk_t_shape = jax.ShapeDtypeStruct((n_tok // MOBA_BLOCK, A_WIDTH, MOBA_BLOCK), _BF16)
    qt, ka, vt, sga, kmean, ybc, sma = pl.pallas_call(
        _proj_kernel,
        grid=(n_tok // PROJ_ROWS,),
        in_specs=[
            row_spec(d_model),
            _const_spec((1, d_model)),
            _const_spec((d_model, in_width)),
            pos_spec, pos_spec,
            _const_spec((1, B_WIDTH)),
            _const_spec((GMLP_GROUPS, GMLP_ROWS, GMLP_ROWS)),
            _const_spec((GMLP_ROWS, GMLP_GROUPS)),
            mem_spec, mem_spec,
            _const_spec((B_WIDTH, d_model)),
            _const_spec((C_WIDTH, d_model)),
        ],
        out_specs=[
            blk_t_spec, row_spec(A_WIDTH), blk_t_spec, row_spec(A_WIDTH),
            pl.BlockSpec((None, blocks_per_tile, A_WIDTH), lambda i: (i, 0, 0)),
            row_spec(d_model), row_spec(d_model),
        ],
        out_shape=[
            blk_t_shape,
            jax.ShapeDtypeStruct((n_tok, A_WIDTH), _BF16),
            blk_t_shape,
            jax.ShapeDtypeStruct((n_tok, A_WIDTH), _BF16),
            jax.ShapeDtypeStruct((n_tok // PROJ_ROWS, blocks_per_tile, A_WIDTH), _F32),
            jax.ShapeDtypeStruct((n_tok, d_model), _BF16),
            jax.ShapeDtypeStruct((n_tok, d_model), _BF16),
        ],
        scratch_shapes=[
            pltpu.VMEM((PROJ_ROWS, B_WIDTH), _BF16),
            pltpu.VMEM((PROJ_ROWS, C_WIDTH), _BF16),
        ],
        compiler_params=params(dimension_semantics=("arbitrary",)),
        name="proj_branches",
    )(x2, norm_g[0].reshape(1, d_model), w_in_b, cos_t, sin_t,
      gmlp_ln_g[0].reshape(1, B_WIDTH), wsp_bd, bsp_t, mk, mv,
      w_branch_b[0].astype(_BF16), w_branch_c[0].astype(_BF16))

    kmean = kmean.reshape(batch, n_blocks, A_WIDTH)
    k3 = ka.reshape(batch, seq, A_WIDTH)
    vt4 = vt.reshape(batch, n_blocks, A_WIDTH, MOBA_BLOCK)
    tile_spec = lambda w: pl.BlockSpec((MOBA_BLOCK, w), lambda b, i: (b * n_blocks + i, 0))
    out = pl.pallas_call(
        _moba_out_kernel,
        grid=(batch, n_blocks),
        in_specs=[
            pl.BlockSpec((None, A_WIDTH, MOBA_BLOCK), lambda b, i: (b * n_blocks + i, 0, 0)),
            pl.BlockSpec((None, seq, A_WIDTH), lambda b, i: (b, 0, 0),
                         pipeline_mode=pl.Buffered(1)),
            pl.BlockSpec((None, n_blocks, A_WIDTH, MOBA_BLOCK), lambda b, i: (b, 0, 0, 0),
                         pipeline_mode=pl.Buffered(1)),
            pl.BlockSpec((None, n_blocks, A_WIDTH), lambda b, i: (b, 0, 0)),
            tile_spec(A_WIDTH), tile_spec(d_model), tile_spec(d_model), tile_spec(d_model),
            _const_spec((A_WIDTH, d_model)),
            _const_spec((d_model, d_model)),
            _const_spec((1, d_model)),
        ],
        out_specs=tile_spec(d_model),
        out_shape=jax.ShapeDtypeStruct((n_tok, d_model), _F32),
        scratch_shapes=[
            pltpu.VMEM((MOBA_HEADS, AUG_DIM, MOBA_BLOCK), _BF16),
            pltpu.VMEM((MOBA_HEADS, MOBA_BLOCK, MOBA_BLOCK), _F32),
            pltpu.VMEM((MOBA_HEADS, MOBA_BLOCK, MOBA_BLOCK), _F32),
            pltpu.VMEM((MOBA_HEADS, MOBA_BLOCK, MOBA_BLOCK), _F32),
            pltpu.VMEM((MOBA_HEADS, 1, MOBA_BLOCK), _F32),
            pltpu.VMEM((MOBA_HEADS, 1, MOBA_BLOCK), _F32),
            pltpu.VMEM((MOBA_HEADS, HEAD_DIM, MOBA_BLOCK), _F32),
            pltpu.VMEM((MOBA_BLOCK, A_WIDTH), _BF16),
        ],
        compiler_params=params(dimension_semantics=("arbitrary", "arbitrary")),
        name="moba_merge_out",
    )(qt, k3, vt4, kmean, sga, ybc, sma, x2,
      w_branch_a[0].astype(_BF16), w_out[0].astype(_BF16), final_norm_g.reshape(1, d_model))
    return out.reshape(batch, seq, d_model)
```

```python
import functools
import math

import jax
import jax.numpy as jnp
from jax import lax
from jax.experimental import pallas as pl
from jax.experimental.pallas import tpu as pltpu

HEAD_DIM = 128
MOBA_HEADS = 6
MOBA_BLOCK = 256
MOBA_TOPK = 3
GMLP_GROUPS = 6
GMLP_CHUNK = 128
XATTN_HEADS = 4
ROPE_THETA = 500000.0
ROT_DIM = HEAD_DIM // 4
EPS = 1e-6

A_WIDTH = MOBA_HEADS * HEAD_DIM
B_WIDTH = GMLP_GROUPS * HEAD_DIM
C_WIDTH = XATTN_HEADS * HEAD_DIM

V7X_VMEM_LIMIT_BYTES = 58 * 1024 * 1024

PROJ_ROWS = 512
GMLP_ROWS = 2 * GMLP_CHUNK

QK_LOG2_SCALE = HEAD_DIM ** -0.5 * math.log2(math.e)
MASK_BIAS = -1e30
AUG_DIM = 2 * HEAD_DIM

_BF16 = jnp.bfloat16
_F32 = jnp.float32
_NEG_INF = float("-inf")


def _rms_norm(x, g):
    return x * lax.rsqrt(jnp.mean(x * x, axis=-1, keepdims=True) + EPS) * g


def _silu(x):
    return x * jax.nn.sigmoid(x)


def _dot(a, b):
    return jnp.dot(a, b, preferred_element_type=_F32)


def _dot_nt(a, b):
    return lax.dot_general(a, b, (((1,), (1,)), ((), ())), preferred_element_type=_F32)


def _mem_kv_kernel(mem_ref, g_ref, w_ref, mk_ref, mv_ref):
    mn = _rms_norm(mem_ref[...], g_ref[...]).astype(_BF16)
    kv = _dot(mn, w_ref[...])
    mk_ref[...] = kv[:, :C_WIDTH].astype(_BF16)
    mv_ref[...] = kv[:, C_WIDTH:].astype(_BF16)


def _proj_kernel(x_ref, g_ref, w_ref, cos_ref, sin_ref, lng_ref, wsp_ref, bsp_ref,
                 mk_ref, mv_ref, wb_ref, wc_ref,
                 qt_ref, k_ref, vt_ref, sga_ref, kmean_ref, ybc_ref, sma_ref,
                 outb_scr, outc_scr):
    rows = x_ref.shape[0]
    d_model = x_ref.shape[1]
    h = _rms_norm(x_ref[...], g_ref[...]).astype(_BF16)

    off = [0]

    def proj(width):
        a = off[0]
        off[0] = a + width
        return _dot(h, w_ref[:, a:a + width])

    cos = cos_ref[...]
    sin = sin_ref[...]
    lane = lax.broadcasted_iota(jnp.int32, (rows, HEAD_DIM), 1)
    first_half = lane < (ROT_DIM // 2)

    def rope(t):
        swapped = jnp.where(first_half,
                            pltpu.roll(t, HEAD_DIM - ROT_DIM // 2, 1),
                            pltpu.roll(t, ROT_DIM // 2, 1))
        return t * cos + swapped * sin

    qa = proj(A_WIDTH)
    for hd in range(MOBA_HEADS):
        sl = slice(hd * HEAD_DIM, (hd + 1) * HEAD_DIM)
        qh = rope(qa[:, sl]) * QK_LOG2_SCALE
        for r in range(rows // MOBA_BLOCK):
            rs = slice(r * MOBA_BLOCK, (r + 1) * MOBA_BLOCK)
            qt_ref[r, sl, :] = qh[rs].T.astype(_BF16)
    ka = proj(A_WIDTH)
    for hd in range(MOBA_HEADS):
        sl = slice(hd * HEAD_DIM, (hd + 1) * HEAD_DIM)
        kh = rope(ka[:, sl])
        k_ref[:, sl] = kh.astype(_BF16)
        kmean_ref[:, sl] = jnp.mean(
            kh.reshape(rows // MOBA_BLOCK, MOBA_BLOCK, HEAD_DIM), axis=1)
    va = proj(A_WIDTH)
    for r in range(rows // MOBA_BLOCK):
        rs = slice(r * MOBA_BLOCK, (r + 1) * MOBA_BLOCK)
        vt_ref[r] = va[rs].T.astype(_BF16)
    sga_ref[...] = _silu(proj(A_WIDTH)).astype(_BF16)

    ub = proj(B_WIDTH)
    vb = proj(B_WIDTH)
    gb = proj(B_WIDTH)
    mu = jnp.mean(vb, axis=-1, keepdims=True)
    var = jnp.mean(jnp.square(vb - mu), axis=-1, keepdims=True)
    vn = ((vb - mu) * lax.rsqrt(var + EPS) * lng_ref[...]).astype(_BF16)
    gate_b = ub * _silu(gb)
    r_i = lax.broadcasted_iota(jnp.int32, (GMLP_ROWS, GMLP_ROWS), 0)
    c_i = lax.broadcasted_iota(jnp.int32, (GMLP_ROWS, GMLP_ROWS), 1)
    causal = r_i >= c_i
    bsp = bsp_ref[...]
    for g in range(GMLP_GROUPS):
        sl = slice(g * HEAD_DIM, (g + 1) * HEAD_DIM)
        wg = jnp.where(causal, wsp_ref[g], 0.0).astype(_BF16)
        bias = bsp[:, g:g + 1]
        for r in range(rows // GMLP_ROWS):
            rs = slice(r * GMLP_ROWS, (r + 1) * GMLP_ROWS)
            mixed = _dot(wg, vn[rs, sl]) + bias
            outb_scr[rs, sl] = (gate_b[rs, sl] * mixed).astype(_BF16)

    qc = proj(C_WIDTH).astype(_BF16)
    gc = proj(C_WIDTH)
    scale = HEAD_DIM ** -0.5
    for hd in range(XATTN_HEADS):
        sl = slice(hd * HEAD_DIM, (hd + 1) * HEAD_DIM)
        s = _dot_nt(qc[:, sl], mk_ref[:, sl]) * scale
        m = jnp.max(s, axis=-1, keepdims=True)
        p = jnp.exp(s - m)
        l = jnp.sum(p, axis=-1, keepdims=True)
        o = _dot(p.astype(_BF16), mv_ref[:, sl]) / l
        outc_scr[:, sl] = (o * _silu(gc[:, sl])).astype(_BF16)

    sma_ref[...] = jax.nn.sigmoid(proj(d_model)).astype(_BF16)
    yb = jax.nn.sigmoid(proj(d_model)) * _dot(outb_scr[...], wb_ref[...])
    yc = jax.nn.sigmoid(proj(d_model)) * _dot(outc_scr[...], wc_ref[...])
    ybc_ref[...] = (yb + yc).astype(_BF16)


def _moba_out_kernel(qt_ref, k_ref, vt_ref, kmean_ref, sga_ref, ybc_ref, sma_ref, x_ref,
                     wa_ref, wo_ref, fg_ref, o_ref,
                     qaug_scr, sa_scr, sb_scr, sd_scr, m_scr, l_scr, acc_scr, outa_scr):
    t = pl.program_id(0)
    n_tiles = pl.num_programs(0) - 1
    n_blocks = kmean_ref.shape[0]
    own = jnp.where(t < n_tiles, t % n_blocks, 0)
    tq = qt_ref.shape[1]

    @pl.when(t == 0)
    def _():
        outa_scr[...] = jnp.zeros(outa_scr.shape, _BF16)
    blk = lax.broadcasted_iota(jnp.int32, (n_blocks, tq), 0)
    valid = blk < own
    key_i = lax.broadcasted_iota(jnp.int32, (MOBA_BLOCK, HEAD_DIM), 0)
    qry_i = lax.broadcasted_iota(jnp.int32, (MOBA_BLOCK, HEAD_DIM), 1)
    own_start = pl.multiple_of(own * MOBA_BLOCK, MOBA_BLOCK)

    for hd in range(MOBA_HEADS):
        sl = slice(hd * HEAD_DIM, (hd + 1) * HEAD_DIM)
        sd_scr[hd] = _dot(k_ref[pl.ds(own_start, MOBA_BLOCK), sl], qt_ref[sl, :])
        m_scr[hd] = jnp.full((1, tq), _NEG_INF, _F32)
        l_scr[hd] = jnp.zeros((1, tq), _F32)
        acc_scr[hd] = jnp.zeros((HEAD_DIM, tq), _F32)

    for hd in range(MOBA_HEADS):
        sl = slice(hd * HEAD_DIM, (hd + 1) * HEAD_DIM)
        qt = qt_ref[sl, :]

        gs = _dot(kmean_ref[:, sl].astype(_BF16), qt)
        gs = jnp.where(valid, gs, _NEG_INF)
        sel = jnp.zeros((n_blocks, tq), dtype=jnp.bool_)
        for _ in range(MOBA_TOPK):
            top = jnp.max(gs, axis=0, keepdims=True)
            idx = jnp.min(jnp.where(gs == top, blk, n_blocks), axis=0, keepdims=True)
            pick = blk == idx
            sel = jnp.logical_or(sel, pick)
            gs = jnp.where(pick, _NEG_INF, gs)
        bias = jnp.where(jnp.logical_and(sel, valid), 0.0, MASK_BIAS)
        qaug_scr[hd, 0:HEAD_DIM, :] = qt
        qaug_scr[hd, HEAD_DIM:HEAD_DIM + n_blocks, :] = bias.astype(_BF16)
        qaug_scr[hd, HEAD_DIM + n_blocks:AUG_DIM, :] = jnp.zeros(
            (AUG_DIM - HEAD_DIM - n_blocks, tq), _BF16)

    ya = _dot(outa_scr[...], wa_ref[...])
    y = sma_ref[...].astype(_F32) * ya + ybc_ref[...].astype(_F32)
    z = x_ref[...] + _dot(y.astype(_BF16), wo_ref[...])
    o_ref[...] = _rms_norm(z, fg_ref[...])

    pair_lane = lax.broadcasted_iota(jnp.int32, (2 * MOBA_BLOCK, HEAD_DIM), 1)
    pair_half = lax.broadcasted_iota(jnp.int32, (2 * MOBA_BLOCK, HEAD_DIM), 0) // MOBA_BLOCK

    def masked_scores_pair(j, hd):
        sl = slice(hd * HEAD_DIM, (hd + 1) * HEAD_DIM)
        start = pl.multiple_of(j * MOBA_BLOCK, MOBA_BLOCK)
        onehot = jnp.where(pair_lane == j + pair_half, 1.0, 0.0).astype(_BF16)
        kaug = jnp.concatenate([k_ref[pl.ds(start, 2 * MOBA_BLOCK), sl], onehot], axis=1)
        return _dot(kaug, qaug_scr[hd])

    def softmax_pv(hd, blocks, causal=False):
        sl = slice(hd * HEAD_DIM, (hd + 1) * HEAD_DIM)
        ps, alphas = [[] for _ in blocks], []
        for half in range(tq // HEAD_DIM):
            cs = slice(half * HEAD_DIM, (half + 1) * HEAD_DIM)
            m_old = m_scr[hd, :, cs]
            m_new = m_old
            for _, s_ref in blocks:
                s = s_ref[hd, :, cs]
                if causal:
                    s = jnp.where(key_i <= qry_i + half * HEAD_DIM, s, _NEG_INF)
                m_new = jnp.maximum(m_new, jnp.max(s, axis=0, keepdims=True))
            alpha = jnp.exp2(m_old - m_new)
            l_new = alpha * l_scr[hd, :, cs]
            for b, (_, s_ref) in enumerate(blocks):
                s = s_ref[hd, :, cs]
                if causal:
                    s = jnp.where(key_i <= qry_i + half * HEAD_DIM, s, _NEG_INF)
                p = jnp.exp2(s - m_new)
                l_new = l_new + jnp.sum(p, axis=0, keepdims=True)
                ps[b].append(p.astype(_BF16))
            m_scr[hd, :, cs] = m_new
            l_scr[hd, :, cs] = l_new
            alphas.append(alpha)
        pv = None
        for (j, _), p_halves in zip(blocks, ps):
            term = _dot(vt_ref[j, sl, :], jnp.concatenate(p_halves, axis=1))
            pv = term if pv is None else pv + term
        for half, alpha in enumerate(alphas):
            cs = slice(half * HEAD_DIM, (half + 1) * HEAD_DIM)
            acc_scr[hd, :, cs] = alpha * acc_scr[hd, :, cs] + pv[:, cs]

    for hd in range(MOBA_HEADS):
        s_pair = masked_scores_pair(0, hd)
        softmax_pv(hd, [(own, sd_scr)], causal=True)
        sa_scr[hd] = s_pair[:MOBA_BLOCK]
        sb_scr[hd] = s_pair[MOBA_BLOCK:]

    def pair_body(jj, carry):
        j0 = 2 * jj
        for hd in range(MOBA_HEADS):
            s_pair = masked_scores_pair(j0 + 2, hd)
            softmax_pv(hd, [(j0, sa_scr), (j0 + 1, sb_scr)])
            sa_scr[hd] = s_pair[:MOBA_BLOCK]
            sb_scr[hd] = s_pair[MOBA_BLOCK:]
        return carry

    lax.fori_loop(0, own // 2, pair_body, 0)

    @pl.when(own % 2 == 1)
    def _():
        for hd in range(MOBA_HEADS):
            softmax_pv(hd, [(own - 1, sa_scr)])

    for hd in range(MOBA_HEADS):
        sl = slice(hd * HEAD_DIM, (hd + 1) * HEAD_DIM)
        o = (acc_scr[hd] / l_scr[hd]).T
        outa_scr[:, sl] = (o * sga_ref[:, sl].astype(_F32)).astype(_BF16)


def _rope_tables(seq):
    inv_freq = ROPE_THETA ** (-jnp.arange(0, ROT_DIM, 2, dtype=_F32) / ROT_DIM)
    ang = jnp.arange(seq).astype(_F32)[:, None] * inv_freq[None, :]
    cos, sin = jnp.cos(ang), jnp.sin(ang)
    rest = HEAD_DIM - ROT_DIM
    cos_t = jnp.concatenate([cos, cos, jnp.ones((seq, rest), _F32)], axis=1)
    sin_t = jnp.concatenate([-sin, sin, jnp.zeros((seq, rest), _F32)], axis=1)
    return cos_t, sin_t


def _const_spec(shape):
    nd = len(shape)
    return pl.BlockSpec(shape, lambda *_: (0,) * nd, pipeline_mode=pl.Buffered(1))


def kernel(x, mem, norm_g, mem_norm_g, final_norm_g, w_in, w_mem_kv, gmlp_ln_g,
           w_spatial, b_spatial, w_branch_a, w_branch_b, w_branch_c, w_out):
    batch, seq, d_model = x.shape
    mem_len = mem.shape[1]
    assert w_in.shape[0] == 1, "single layer"
    assert seq % PROJ_ROWS == 0 and PROJ_ROWS % MOBA_BLOCK == 0 and PROJ_ROWS % GMLP_ROWS == 0
    n_tok = batch * seq
    n_blocks = seq // MOBA_BLOCK
    assert HEAD_DIM + n_blocks <= AUG_DIM and n_blocks % 16 == 0
    in_width = w_in.shape[-1]

    x2 = x.reshape(n_tok, d_model)
    w_in_b = w_in[0].astype(_BF16)
    cos_t, sin_t = _rope_tables(seq)
    eye = jnp.eye(GMLP_ROWS // GMLP_CHUNK, dtype=_F32)
    wsp_bd = jnp.einsum("ab,gts->gatbs", eye, w_spatial[0]).reshape(
        GMLP_GROUPS, GMLP_ROWS, GMLP_ROWS)
    bsp_t = jnp.tile(jnp.transpose(b_spatial[0]), (GMLP_ROWS // GMLP_CHUNK, 1))

    params = functools.partial(pltpu.CompilerParams, vmem_limit_bytes=V7X_VMEM_LIMIT_BYTES)

    mk, mv = pl.pallas_call(
        _mem_kv_kernel,
        grid=(batch,),
        in_specs=[
            pl.BlockSpec((None, mem_len, d_model), lambda b: (b, 0, 0)),
            pl.BlockSpec((1, d_model), lambda b: (0, 0)),
            pl.BlockSpec((d_model, 2 * C_WIDTH), lambda b: (0, 0)),
        ],
        out_specs=[
            pl.BlockSpec((None, mem_len, C_WIDTH), lambda b: (b, 0, 0)),
            pl.BlockSpec((None, mem_len, C_WIDTH), lambda b: (b, 0, 0)),
        ],
        out_shape=[jax.ShapeDtypeStruct((batch, mem_len, C_WIDTH), _BF16)] * 2,
        compiler_params=params(dimension_semantics=("arbitrary",)),
        name="mem_kv",
    )(mem, mem_norm_g[0].reshape(1, d_model), w_mem_kv[0].astype(_BF16))

    tiles_per_seq = seq // PROJ_ROWS
    blocks_per_tile = PROJ_ROWS // MOBA_BLOCK
    row_spec = lambda w: pl.BlockSpec((PROJ_ROWS, w), lambda i: (i, 0))
    blk_t_spec = pl.BlockSpec((blocks_per_tile, A_WIDTH, MOBA_BLOCK), lambda i: (i, 0, 0))
    pos_spec = pl.BlockSpec((PROJ_ROWS, HEAD_DIM), lambda i: (i % tiles_per_seq, 0))
    mem_spec = pl.BlockSpec((None, mem_len, C_WIDTH), lambda i: (i // tiles_per_seq, 0, 0))
    blk_t_shape = jax.ShapeDtypeStruct((n_tok // MOBA_BLOCK, A_WIDTH, MOBA_BLOCK), _BF16)
    qt, ka, vt, sga, kmean, ybc, sma = pl.pallas_call(
        _proj_kernel,
        grid=(n_tok // PROJ_ROWS,),
        in_specs=[
            row_spec(d_model),
            _const_spec((1, d_model)),
            _const_spec((d_model, in_width)),
            pos_spec, pos_spec,
            _const_spec((1, B_WIDTH)),
            _const_spec((GMLP_GROUPS, GMLP_ROWS, GMLP_ROWS)),
            _const_spec((GMLP_ROWS, GMLP_GROUPS)),
            mem_spec, mem_spec,
            _const_spec((B_WIDTH, d_model)),
            _const_spec((C_WIDTH, d_model)),
        ],
        out_specs=[
            blk_t_spec, row_spec(A_WIDTH), blk_t_spec, row_spec(A_WIDTH),
            pl.BlockSpec((None, blocks_per_tile, A_WIDTH), lambda i: (i, 0, 0)),
            row_spec(d_model), row_spec(d_model),
        ],
        out_shape=[
            blk_t_shape,
            jax.ShapeDtypeStruct((n_tok, A_WIDTH), _BF16),
            blk_t_shape,
            jax.ShapeDtypeStruct((n_tok, A_WIDTH), _BF16),
            jax.ShapeDtypeStruct((n_tok // PROJ_ROWS, blocks_per_tile, A_WIDTH), _F32),
            jax.ShapeDtypeStruct((n_tok, d_model), _BF16),
            jax.ShapeDtypeStruct((n_tok, d_model), _BF16),
        ],
        scratch_shapes=[
            pltpu.VMEM((PROJ_ROWS, B_WIDTH), _BF16),
            pltpu.VMEM((PROJ_ROWS, C_WIDTH), _BF16),
        ],
        compiler_params=params(dimension_semantics=("arbitrary",)),
        name="proj_branches",
    )(x2, norm_g[0].reshape(1, d_model), w_in_b, cos_t, sin_t,
      gmlp_ln_g[0].reshape(1, B_WIDTH), wsp_bd, bsp_t, mk, mv,
      w_branch_b[0].astype(_BF16), w_branch_c[0].astype(_BF16))

    kmean = kmean.reshape(batch, n_blocks, A_WIDTH)
    k3 = ka.reshape(batch, seq, A_WIDTH)
    vt4 = vt.reshape(batch, n_blocks, A_WIDTH, MOBA_BLOCK)
    n_tiles = batch * n_blocks
    attn_tile = lambda t: jnp.minimum(t, n_tiles - 1)
    done_tile = lambda t: jnp.maximum(t - 1, 0)
    attn_spec = lambda w: pl.BlockSpec((MOBA_BLOCK, w), lambda t: (attn_tile(t), 0))
    done_spec = lambda w: pl.BlockSpec((MOBA_BLOCK, w), lambda t: (done_tile(t), 0))
    out = pl.pallas_call(
        _moba_out_kernel,
        grid=(n_tiles + 1,),
        in_specs=[
            pl.BlockSpec((None, A_WIDTH, MOBA_BLOCK), lambda t: (attn_tile(t), 0, 0)),
            pl.BlockSpec((None, seq, A_WIDTH), lambda t: (attn_tile(t) // n_blocks, 0, 0),
                         pipeline_mode=pl.Buffered(1)),
            pl.BlockSpec((None, n_blocks, A_WIDTH, MOBA_BLOCK),
                         lambda t: (attn_tile(t) // n_blocks, 0, 0, 0),
                         pipeline_mode=pl.Buffered(1)),
            pl.BlockSpec((None, n_blocks, A_WIDTH), lambda t: (attn_tile(t) // n_blocks, 0, 0)),
            attn_spec(A_WIDTH), done_spec(d_model), done_spec(d_model), done_spec(d_model),
            _const_spec((A_WIDTH, d_model)),
            _const_spec((d_model, d_model)),
            _const_spec((1, d_model)),
        ],
        out_specs=done_spec(d_model),
        out_shape=jax.ShapeDtypeStruct((n_tok, d_model), _F32),
        scratch_shapes=[
            pltpu.VMEM((MOBA_HEADS, AUG_DIM, MOBA_BLOCK), _BF16),
            pltpu.VMEM((MOBA_HEADS, MOBA_BLOCK, MOBA_BLOCK), _F32),
            pltpu.VMEM((MOBA_HEADS, MOBA_BLOCK, MOBA_BLOCK), _F32),
            pltpu.VMEM((MOBA_HEADS, MOBA_BLOCK, MOBA_BLOCK), _F32),
            pltpu.VMEM((MOBA_HEADS, 1, MOBA_BLOCK), _F32),
            pltpu.VMEM((MOBA_HEADS, 1, MOBA_BLOCK), _F32),
            pltpu.VMEM((MOBA_HEADS, HEAD_DIM, MOBA_BLOCK), _F32),
            pltpu.VMEM((MOBA_BLOCK, A_WIDTH), _BF16),
        ],
        compiler_params=params(dimension_semantics=("arbitrary",)),
        name="moba_merge_out",
    )(qt, k3, vt4, kmean, sga, ybc, sma, x2,
      w_branch_a[0].astype(_BF16), w_out[0].astype(_BF16), final_norm_g.reshape(1, d_model))
    return out.reshape(batch, seq, d_model)
```

```python
import functools
import math

import jax
import jax.numpy as jnp
from jax import lax
from jax.experimental import pallas as pl
from jax.experimental.pallas import tpu as pltpu

HEAD_DIM = 128
MOBA_HEADS = 6
MOBA_BLOCK = 256
MOBA_TOPK = 3
GMLP_GROUPS = 6
GMLP_CHUNK = 128
XATTN_HEADS = 4
ROPE_THETA = 500000.0
ROT_DIM = HEAD_DIM // 4
EPS = 1e-6

A_WIDTH = MOBA_HEADS * HEAD_DIM
B_WIDTH = GMLP_GROUPS * HEAD_DIM
C_WIDTH = XATTN_HEADS * HEAD_DIM

V7X_VMEM_LIMIT_BYTES = 58 * 1024 * 1024

PROJ_ROWS = 512
GMLP_ROWS = 2 * GMLP_CHUNK

QK_LOG2_SCALE = HEAD_DIM ** -0.5 * math.log2(math.e)
MASK_BIAS = -1e30
AUG_DIM = 2 * HEAD_DIM
SUM_ROWS = 16
STEPS_PER_TRIP = 4

_BF16 = jnp.bfloat16
_F32 = jnp.float32
_NEG_INF = float("-inf")


def _rms_norm(x, g):
    return x * lax.rsqrt(jnp.mean(x * x, axis=-1, keepdims=True) + EPS) * g


def _silu(x):
    return x * jax.nn.sigmoid(x)


def _dot(a, b):
    return jnp.dot(a, b, preferred_element_type=_F32)


def _dot_nt(a, b):
    return lax.dot_general(a, b, (((1,), (1,)), ((), ())), preferred_element_type=_F32)


def _mem_kv_kernel(mem_ref, g_ref, w_ref, mk_ref, mv_ref):
    mn = _rms_norm(mem_ref[...], g_ref[...]).astype(_BF16)
    kv = _dot(mn, w_ref[...])
    mk_ref[...] = kv[:, :C_WIDTH].astype(_BF16)
    mv_ref[...] = kv[:, C_WIDTH:].astype(_BF16)


def _proj_kernel(x_ref, g_ref, w_ref, cos_ref, sin_ref, lng_ref, wsp_ref, bsp_ref,
                 mk_ref, mv_ref, wb_ref, wc_ref,
                 qt_ref, k_ref, vt_ref, sga_ref, kmean_ref, ybc_ref, sma_ref,
                 outb_scr, outc_scr):
    rows = x_ref.shape[0]
    d_model = x_ref.shape[1]
    h = _rms_norm(x_ref[...], g_ref[...]).astype(_BF16)

    off = [0]

    def proj(width):
        a = off[0]
        off[0] = a + width
        return _dot(h, w_ref[:, a:a + width])

    cos = cos_ref[...]
    sin = sin_ref[...]
    lane = lax.broadcasted_iota(jnp.int32, (rows, HEAD_DIM), 1)
    first_half = lane < (ROT_DIM // 2)

    def rope(t):
        swapped = jnp.where(first_half,
                            pltpu.roll(t, HEAD_DIM - ROT_DIM // 2, 1),
                            pltpu.roll(t, ROT_DIM // 2, 1))
        return t * cos + swapped * sin

    qa = proj(A_WIDTH)
    for hd in range(MOBA_HEADS):
        sl = slice(hd * HEAD_DIM, (hd + 1) * HEAD_DIM)
        qh = rope(qa[:, sl]) * QK_LOG2_SCALE
        for r in range(rows // MOBA_BLOCK):
            rs = slice(r * MOBA_BLOCK, (r + 1) * MOBA_BLOCK)
            qt_ref[r, sl, :] = qh[rs].T.astype(_BF16)
    ka = proj(A_WIDTH)
    for hd in range(MOBA_HEADS):
        sl = slice(hd * HEAD_DIM, (hd + 1) * HEAD_DIM)
        kh = rope(ka[:, sl])
        k_ref[:, sl] = kh.astype(_BF16)
        kmean_ref[:, sl] = jnp.mean(
            kh.reshape(rows // MOBA_BLOCK, MOBA_BLOCK, HEAD_DIM), axis=1)
    va = proj(A_WIDTH)
    for r in range(rows // MOBA_BLOCK):
        rs = slice(r * MOBA_BLOCK, (r + 1) * MOBA_BLOCK)
        vt_ref[r] = va[rs].T.astype(_BF16)
    sga_ref[...] = _silu(proj(A_WIDTH)).astype(_BF16)

    ub = proj(B_WIDTH)
    vb = proj(B_WIDTH)
    gb = proj(B_WIDTH)
    mu = jnp.mean(vb, axis=-1, keepdims=True)
    var = jnp.mean(jnp.square(vb - mu), axis=-1, keepdims=True)
    vn = ((vb - mu) * lax.rsqrt(var + EPS) * lng_ref[...]).astype(_BF16)
    gate_b = ub * _silu(gb)
    r_i = lax.broadcasted_iota(jnp.int32, (GMLP_ROWS, GMLP_ROWS), 0)
    c_i = lax.broadcasted_iota(jnp.int32, (GMLP_ROWS, GMLP_ROWS), 1)
    causal = r_i >= c_i
    bsp = bsp_ref[...]
    for g in range(GMLP_GROUPS):
        sl = slice(g * HEAD_DIM, (g + 1) * HEAD_DIM)
        wg = jnp.where(causal, wsp_ref[g], 0.0).astype(_BF16)
        bias = bsp[:, g:g + 1]
        for r in range(rows // GMLP_ROWS):
            rs = slice(r * GMLP_ROWS, (r + 1) * GMLP_ROWS)
            mixed = _dot(wg, vn[rs, sl]) + bias
            outb_scr[rs, sl] = (gate_b[rs, sl] * mixed).astype(_BF16)

    qc = proj(C_WIDTH).astype(_BF16)
    gc = proj(C_WIDTH)
    scale = HEAD_DIM ** -0.5
    for hd in range(XATTN_HEADS):
        sl = slice(hd * HEAD_DIM, (hd + 1) * HEAD_DIM)
        s = _dot_nt(qc[:, sl], mk_ref[:, sl]) * scale
        m = jnp.max(s, axis=-1, keepdims=True)
        p = jnp.exp(s - m)
        l = jnp.sum(p, axis=-1, keepdims=True)
        o = _dot(p.astype(_BF16), mv_ref[:, sl]) / l
        outc_scr[:, sl] = (o * _silu(gc[:, sl])).astype(_BF16)

    sma_ref[...] = jax.nn.sigmoid(proj(d_model)).astype(_BF16)
    yb = jax.nn.sigmoid(proj(d_model)) * _dot(outb_scr[...], wb_ref[...])
    yc = jax.nn.sigmoid(proj(d_model)) * _dot(outc_scr[...], wc_ref[...])
    ybc_ref[...] = (yb + yc).astype(_BF16)


def _moba_out_kernel(qt_ref, k_ref, vt_ref, kmean_ref, sga_ref, ybc_ref, sma_ref, x_ref,
                     wa_ref, wo_ref, fg_ref, o_ref,
                     qaug_scr, sa_scr, sb_scr, sd_scr, p_scr, m_scr, alpha_scr, acc_scr, outa_scr):
    t = pl.program_id(0)
    n_tiles = pl.num_programs(0) - 1
    n_blocks = kmean_ref.shape[0]
    own = jnp.where(t < n_tiles, t % n_blocks, 0)
    tq = qt_ref.shape[1]

    @pl.when(t == 0)
    def _():
        outa_scr[...] = jnp.zeros(outa_scr.shape, _BF16)
    blk = lax.broadcasted_iota(jnp.int32, (n_blocks, tq), 0)
    valid = blk < own
    key_i = lax.broadcasted_iota(jnp.int32, (MOBA_BLOCK, HEAD_DIM), 0)
    qry_i = lax.broadcasted_iota(jnp.int32, (MOBA_BLOCK, HEAD_DIM), 1)
    own_start = pl.multiple_of(own * MOBA_BLOCK, MOBA_BLOCK)

    for hd in range(MOBA_HEADS):
        sl = slice(hd * HEAD_DIM, (hd + 1) * HEAD_DIM)
        sd_scr[hd] = _dot(k_ref[pl.ds(own_start, MOBA_BLOCK), sl], qt_ref[sl, :])
        m_scr[hd] = jnp.full((1, tq), _NEG_INF, _F32)
        acc_scr[hd] = jnp.zeros((HEAD_DIM + SUM_ROWS, tq), _F32)

    for hd in range(MOBA_HEADS):
        sl = slice(hd * HEAD_DIM, (hd + 1) * HEAD_DIM)
        qt = qt_ref[sl, :]

        gs = _dot(kmean_ref[:, sl].astype(_BF16), qt)
        gs = jnp.where(valid, gs, _NEG_INF)
        sel = jnp.zeros((n_blocks, tq), dtype=jnp.bool_)
        for _ in range(MOBA_TOPK):
            top = jnp.max(gs, axis=0, keepdims=True)
            idx = jnp.min(jnp.where(gs == top, blk, n_blocks), axis=0, keepdims=True)
            pick = blk == idx
            sel = jnp.logical_or(sel, pick)
            gs = jnp.where(pick, _NEG_INF, gs)
        bias = jnp.where(jnp.logical_and(sel, valid), 0.0, MASK_BIAS)
        qaug_scr[hd, 0:HEAD_DIM, :] = qt
        qaug_scr[hd, HEAD_DIM:HEAD_DIM + n_blocks, :] = bias.astype(_BF16)
        qaug_scr[hd, HEAD_DIM + n_blocks:AUG_DIM, :] = jnp.zeros(
            (AUG_DIM - HEAD_DIM - n_blocks, tq), _BF16)

    ya = _dot(outa_scr[...], wa_ref[...])
    y = sma_ref[...].astype(_F32) * ya + ybc_ref[...].astype(_F32)
    z = x_ref[...] + _dot(y.astype(_BF16), wo_ref[...])
    o_ref[...] = _rms_norm(z, fg_ref[...])

    pair_lane = lax.broadcasted_iota(jnp.int32, (2 * MOBA_BLOCK, HEAD_DIM), 1)
    pair_half = lax.broadcasted_iota(jnp.int32, (2 * MOBA_BLOCK, HEAD_DIM), 0) // MOBA_BLOCK

    def block_onehot_pair(j):
        return jnp.where(pair_lane == j + pair_half, 1.0, 0.0).astype(_BF16)

    def masked_scores_pair(j, hd, onehot):
        sl = slice(hd * HEAD_DIM, (hd + 1) * HEAD_DIM)
        start = pl.multiple_of(j * MOBA_BLOCK, MOBA_BLOCK)
        kaug = jnp.concatenate([k_ref[pl.ds(start, 2 * MOBA_BLOCK), sl], onehot], axis=1)
        return _dot(kaug, qaug_scr[hd])

    def softmax_stage(hd, score_halves, causal=False):
        ps, alphas = [[] for _ in score_halves], []
        for half in range(tq // HEAD_DIM):
            cs = slice(half * HEAD_DIM, (half + 1) * HEAD_DIM)
            m_old = m_scr[hd, :, cs]
            m_new = m_old
            for get in score_halves:
                s = get(cs)
                if causal:
                    s = jnp.where(key_i <= qry_i + half * HEAD_DIM, s, _NEG_INF)
                m_new = jnp.maximum(m_new, jnp.max(s, axis=0, keepdims=True))
            alphas.append(jnp.exp2(m_old - m_new))
            for b, get in enumerate(score_halves):
                s = get(cs)
                if causal:
                    s = jnp.where(key_i <= qry_i + half * HEAD_DIM, s, _NEG_INF)
                ps[b].append(jnp.exp2(s - m_new).astype(_BF16))
            m_scr[hd, :, cs] = m_new
        return [jnp.concatenate(p_halves, axis=1) for p_halves in ps], alphas

    def pv_stage(hd, blocks, alphas):
        sl = slice(hd * HEAD_DIM, (hd + 1) * HEAD_DIM)
        ones_rows = jnp.ones((SUM_ROWS, MOBA_BLOCK), _BF16)
        pv = None
        for j, p in blocks:
            term = _dot(jnp.concatenate([vt_ref[j, sl, :], ones_rows], axis=0), p)
            pv = term if pv is None else pv + term
        for half, alpha in enumerate(alphas):
            cs = slice(half * HEAD_DIM, (half + 1) * HEAD_DIM)
            acc_scr[hd, :, cs] = alpha * acc_scr[hd, :, cs] + pv[:, cs]

    def save_probs(hd, ps, alphas):
        p_scr[hd, :MOBA_BLOCK] = ps[0]
        p_scr[hd, MOBA_BLOCK:] = ps[1]
        for half, alpha in enumerate(alphas):
            alpha_scr[hd, :, half * HEAD_DIM:(half + 1) * HEAD_DIM] = alpha

    def saved_alphas(hd):
        return [alpha_scr[hd, :, half * HEAD_DIM:(half + 1) * HEAD_DIM]
                for half in range(tq // HEAD_DIM)]

    last_pair_start = n_blocks - 2
    onehot0 = block_onehot_pair(0)
    onehot1 = block_onehot_pair(2)
    for hd in range(MOBA_HEADS):
        s0 = masked_scores_pair(0, hd, onehot0)
        ps, alphas = softmax_stage(hd, [lambda cs, hd=hd: sd_scr[hd, :, cs]], causal=True)
        pv_stage(hd, [(own, ps[0])], alphas)
        save_probs(hd, *softmax_stage(
            hd, [lambda cs, s0=s0: s0[:MOBA_BLOCK, cs], lambda cs, s0=s0: s0[MOBA_BLOCK:, cs]]))
        s1 = masked_scores_pair(2, hd, onehot1)
        sa_scr[hd] = s1[:MOBA_BLOCK]
        sb_scr[hd] = s1[MOBA_BLOCK:]

    def pipeline_step(t, pending):
        j0 = 2 * t
        j_score = jnp.minimum(j0 + 4, last_pair_start)
        onehot = block_onehot_pair(j_score)
        out = []
        for hd in range(MOBA_HEADS):
            if pending is None:
                ps = [p_scr[hd, :MOBA_BLOCK], p_scr[hd, MOBA_BLOCK:]]
                alphas = saved_alphas(hd)
            else:
                ps, alphas = pending[hd]
            pv_stage(hd, [(j0, ps[0]), (j0 + 1, ps[1])], alphas)
            s_next = masked_scores_pair(j_score, hd, onehot)
            out.append(softmax_stage(
                hd, [lambda cs, hd=hd: sa_scr[hd, :, cs], lambda cs, hd=hd: sb_scr[hd, :, cs]]))
            sa_scr[hd] = s_next[:MOBA_BLOCK]
            sb_scr[hd] = s_next[MOBA_BLOCK:]
        return out

    def pipeline_steps(t0, count):
        pending = None
        for u in range(count):
            pending = pipeline_step(t0 + u, pending)
        for hd in range(MOBA_HEADS):
            save_probs(hd, *pending[hd])

    n_steps = own // 2

    def trip_body(g, carry):
        pipeline_steps(STEPS_PER_TRIP * g, STEPS_PER_TRIP)
        return carry

    lax.fori_loop(0, n_steps // STEPS_PER_TRIP, trip_body, 0)
    done = (n_steps // STEPS_PER_TRIP) * STEPS_PER_TRIP
    width = STEPS_PER_TRIP // 2
    while width >= 1:
        @pl.when((n_steps & width) != 0)
        def _(done=done, width=width):
            pipeline_steps(done, width)
        done = done + (n_steps & width)
        width //= 2

    @pl.when(own % 2 == 1)
    def _():
        for hd in range(MOBA_HEADS):
            pv_stage(hd, [(own - 1, p_scr[hd, :MOBA_BLOCK])], saved_alphas(hd))

    for hd in range(MOBA_HEADS):
        sl = slice(hd * HEAD_DIM, (hd + 1) * HEAD_DIM)
        o = (acc_scr[hd, :HEAD_DIM, :] / acc_scr[hd, HEAD_DIM:HEAD_DIM + 1, :]).T
        outa_scr[:, sl] = (o * sga_ref[:, sl].astype(_F32)).astype(_BF16)


def _rope_tables(seq):
    inv_freq = ROPE_THETA ** (-jnp.arange(0, ROT_DIM, 2, dtype=_F32) / ROT_DIM)
    ang = jnp.arange(seq).astype(_F32)[:, None] * inv_freq[None, :]
    cos, sin = jnp.cos(ang), jnp.sin(ang)
    rest = HEAD_DIM - ROT_DIM
    cos_t = jnp.concatenate([cos, cos, jnp.ones((seq, rest), _F32)], axis=1)
    sin_t = jnp.concatenate([-sin, sin, jnp.zeros((seq, rest), _F32)], axis=1)
    return cos_t, sin_t


def _const_spec(shape):
    nd = len(shape)
    return pl.BlockSpec(shape, lambda *_: (0,) * nd, pipeline_mode=pl.Buffered(1))


def kernel(x, mem, norm_g, mem_norm_g, final_norm_g, w_in, w_mem_kv, gmlp_ln_g,
           w_spatial, b_spatial, w_branch_a, w_branch_b, w_branch_c, w_out):
    batch, seq, d_model = x.shape
    mem_len = mem.shape[1]
    assert w_in.shape[0] == 1, "single layer"
    assert seq % PROJ_ROWS == 0 and PROJ_ROWS % MOBA_BLOCK == 0 and PROJ_ROWS % GMLP_ROWS == 0
    n_tok = batch * seq
    n_blocks = seq // MOBA_BLOCK
    assert HEAD_DIM + n_blocks <= AUG_DIM and n_blocks % 16 == 0
    in_width = w_in.shape[-1]

    x2 = x.reshape(n_tok, d_model)
    w_in_b = w_in[0].astype(_BF16)
    cos_t, sin_t = _rope_tables(seq)
    eye = jnp.eye(GMLP_ROWS // GMLP_CHUNK, dtype=_F32)
    wsp_bd = jnp.einsum("ab,gts->gatbs", eye, w_spatial[0]).reshape(
        GMLP_GROUPS, GMLP_ROWS, GMLP_ROWS)
    bsp_t = jnp.tile(jnp.transpose(b_spatial[0]), (GMLP_ROWS // GMLP_CHUNK, 1))

    params = functools.partial(pltpu.CompilerParams, vmem_limit_bytes=V7X_VMEM_LIMIT_BYTES)

    mk, mv = pl.pallas_call(
        _mem_kv_kernel,
        grid=(batch,),
        in_specs=[
            pl.BlockSpec((None, mem_len, d_model), lambda b: (b, 0, 0)),
            pl.BlockSpec((1, d_model), lambda b: (0, 0)),
            pl.BlockSpec((d_model, 2 * C_WIDTH), lambda b: (0, 0)),
        ],
        out_specs=[
            pl.BlockSpec((None, mem_len, C_WIDTH), lambda b: (b, 0, 0)),
            pl.BlockSpec((None, mem_len, C_WIDTH), lambda b: (b, 0, 0)),
        ],
        out_shape=[jax.ShapeDtypeStruct((batch, mem_len, C_WIDTH), _BF16)] * 2,
        compiler_params=params(dimension_semantics=("arbitrary",)),
        name="mem_kv",
    )(mem, mem_norm_g[0].reshape(1, d_model), w_mem_kv[0].astype(_BF16))

    tiles_per_seq = seq // PROJ_ROWS
    blocks_per_tile = PROJ_ROWS // MOBA_BLOCK
    row_spec = lambda w: pl.BlockSpec((PROJ_ROWS, w), lambda i: (i, 0))
    blk_t_spec = pl.BlockSpec((blocks_per_tile, A_WIDTH, MOBA_BLOCK), lambda i: (i, 0, 0))
    pos_spec = pl.BlockSpec((PROJ_ROWS, HEAD_DIM), lambda i: (i % tiles_per_seq, 0))
    mem_spec = pl.BlockSpec((None, mem_len, C_WIDTH), lambda i: (i // tiles_per_seq, 0, 0))
    blk_t_shape = jax.ShapeDtypeStruct((n_tok // MOBA_BLOCK, A_WIDTH, MOBA_BLOCK), _BF16)
    qt, ka, vt, sga, kmean, ybc, sma = pl.pallas_call(
        _proj_kernel,
        grid=(n_tok // PROJ_ROWS,),
        in_specs=[
            row_spec(d_model),
            _const_spec((1, d_model)),
            _const_spec((d_model, in_width)),
            pos_spec, pos_spec,
            _const_spec((1, B_WIDTH)),
            _const_spec((GMLP_GROUPS, GMLP_ROWS, GMLP_ROWS)),
            _const_spec((GMLP_ROWS, GMLP_GROUPS)),
            mem_spec, mem_spec,
            _const_spec((B_WIDTH, d_model)),
            _const_spec((C_WIDTH, d_model)),
        ],
        out_specs=[
            blk_t_spec, row_spec(A_WIDTH), blk_t_spec, row_spec(A_WIDTH),
            pl.BlockSpec((None, blocks_per_tile, A_WIDTH), lambda i: (i, 0, 0)),
            row_spec(d_model), row_spec(d_model),
        ],
        out_shape=[
            blk_t_shape,
            jax.ShapeDtypeStruct((n_tok, A_WIDTH), _BF16),
            blk_t_shape,
            jax.ShapeDtypeStruct((n_tok, A_WIDTH), _BF16),
            jax.ShapeDtypeStruct((n_tok // PROJ_ROWS, blocks_per_tile, A_WIDTH), _F32),
            jax.ShapeDtypeStruct((n_tok, d_model), _BF16),
            jax.ShapeDtypeStruct((n_tok, d_model), _BF16),
        ],
        scratch_shapes=[
            pltpu.VMEM((PROJ_ROWS, B_WIDTH), _BF16),
            pltpu.VMEM((PROJ_ROWS, C_WIDTH), _BF16),
        ],
        compiler_params=params(dimension_semantics=("arbitrary",)),
        name="proj_branches",
    )(x2, norm_g[0].reshape(1, d_model), w_in_b, cos_t, sin_t,
      gmlp_ln_g[0].reshape(1, B_WIDTH), wsp_bd, bsp_t, mk, mv,
      w_branch_b[0].astype(_BF16), w_branch_c[0].astype(_BF16))

    kmean = kmean.reshape(batch, n_blocks, A_WIDTH)
    k3 = ka.reshape(batch, seq, A_WIDTH)
    vt4 = vt.reshape(batch, n_blocks, A_WIDTH, MOBA_BLOCK)
    n_tiles = batch * n_blocks
    attn_tile = lambda t: jnp.minimum(t, n_tiles - 1)
    done_tile = lambda t: jnp.maximum(t - 1, 0)
    attn_spec = lambda w: pl.BlockSpec((MOBA_BLOCK, w), lambda t: (attn_tile(t), 0))
    done_spec = lambda w: pl.BlockSpec((MOBA_BLOCK, w), lambda t: (done_tile(t), 0))
    out = pl.pallas_call(
        _moba_out_kernel,
        grid=(n_tiles + 1,),
        in_specs=[
            pl.BlockSpec((None, A_WIDTH, MOBA_BLOCK), lambda t: (attn_tile(t), 0, 0)),
            pl.BlockSpec((None, seq, A_WIDTH), lambda t: (attn_tile(t) // n_blocks, 0, 0),
                         pipeline_mode=pl.Buffered(1)),
            pl.BlockSpec((None, n_blocks, A_WIDTH, MOBA_BLOCK),
                         lambda t: (attn_tile(t) // n_blocks, 0, 0, 0),
                         pipeline_mode=pl.Buffered(1)),
            pl.BlockSpec((None, n_blocks, A_WIDTH), lambda t: (attn_tile(t) // n_blocks, 0, 0)),
            attn_spec(A_WIDTH), done_spec(d_model), done_spec(d_model), done_spec(d_model),
            _const_spec((A_WIDTH, d_model)),
            _const_spec((d_model, d_model)),
            _const_spec((1, d_model)),
        ],
        out_specs=done_spec(d_model),
        out_shape=jax.ShapeDtypeStruct((n_tok, d_model), _F32),
        scratch_shapes=[
            pltpu.VMEM((MOBA_HEADS, AUG_DIM, MOBA_BLOCK), _BF16),
            pltpu.VMEM((MOBA_HEADS, MOBA_BLOCK, MOBA_BLOCK), _F32),
            pltpu.VMEM((MOBA_HEADS, MOBA_BLOCK, MOBA_BLOCK), _F32),
            pltpu.VMEM((MOBA_HEADS, MOBA_BLOCK, MOBA_BLOCK), _F32),
            pltpu.VMEM((MOBA_HEADS, 2 * MOBA_BLOCK, MOBA_BLOCK), _BF16),
            pltpu.VMEM((MOBA_HEADS, 1, MOBA_BLOCK), _F32),
            pltpu.VMEM((MOBA_HEADS, 1, MOBA_BLOCK), _F32),
            pltpu.VMEM((MOBA_HEADS, HEAD_DIM + SUM_ROWS, MOBA_BLOCK), _F32),
            pltpu.VMEM((MOBA_BLOCK, A_WIDTH), _BF16),
        ],
        compiler_params=params(dimension_semantics=("arbitrary",)),
        name="moba_merge_out",
    )(qt, k3, vt4, kmean, sga, ybc, sma, x2,
      w_branch_a[0].astype(_BF16), w_out[0].astype(_BF16), final_norm_g.reshape(1, d_model))
    return out.reshape(batch, seq, d_model)
```

```python
import functools
import math

import jax
import jax.numpy as jnp
from jax import lax
from jax.experimental import pallas as pl
from jax.experimental.pallas import tpu as pltpu

HEAD_DIM = 128
MOBA_HEADS = 6
MOBA_BLOCK = 256
MOBA_TOPK = 3
GMLP_GROUPS = 6
GMLP_CHUNK = 128
XATTN_HEADS = 4
ROPE_THETA = 500000.0
ROT_DIM = HEAD_DIM // 4
EPS = 1e-6

A_WIDTH = MOBA_HEADS * HEAD_DIM
B_WIDTH = GMLP_GROUPS * HEAD_DIM
C_WIDTH = XATTN_HEADS * HEAD_DIM

V7X_VMEM_LIMIT_BYTES = 58 * 1024 * 1024

PROJ_ROWS = 512
GMLP_ROWS = 2 * GMLP_CHUNK

QK_LOG2_SCALE = HEAD_DIM ** -0.5 * math.log2(math.e)
MASK_BIAS = -1e30
AUG_DIM = 2 * HEAD_DIM
SUM_ROWS = 16
BLOCKS_PER_TRIP = 4
SCORE_LOOKAHEAD = 8

_BF16 = jnp.bfloat16
_F32 = jnp.float32
_NEG_INF = float("-inf")


def _rms_norm(x, g):
    return x * lax.rsqrt(jnp.mean(x * x, axis=-1, keepdims=True) + EPS) * g


def _silu(x):
    return x * jax.nn.sigmoid(x)


def _dot(a, b):
    return jnp.dot(a, b, preferred_element_type=_F32)


def _dot_nt(a, b):
    return lax.dot_general(a, b, (((1,), (1,)), ((), ())), preferred_element_type=_F32)


def _mem_kv_kernel(mem_ref, g_ref, w_ref, mk_ref, mv_ref):
    mn = _rms_norm(mem_ref[...], g_ref[...]).astype(_BF16)
    kv = _dot(mn, w_ref[...])
    mk_ref[...] = kv[:, :C_WIDTH].astype(_BF16)
    mv_ref[...] = kv[:, C_WIDTH:].astype(_BF16)


def _proj_kernel(x_ref, g_ref, w_ref, cos_ref, sin_ref, lng_ref, wsp_ref, bsp_ref,
                 mk_ref, mv_ref, wb_ref, wc_ref,
                 qt_ref, k_ref, vt_ref, sga_ref, kmean_ref, ybc_ref, sma_ref,
                 outb_scr, outc_scr):
    rows = x_ref.shape[0]
    d_model = x_ref.shape[1]
    h = _rms_norm(x_ref[...], g_ref[...]).astype(_BF16)

    off = [0]

    def proj(width):
        a = off[0]
        off[0] = a + width
        return _dot(h, w_ref[:, a:a + width])

    cos = cos_ref[...]
    sin = sin_ref[...]
    lane = lax.broadcasted_iota(jnp.int32, (rows, HEAD_DIM), 1)
    first_half = lane < (ROT_DIM // 2)

    def rope(t):
        swapped = jnp.where(first_half,
                            pltpu.roll(t, HEAD_DIM - ROT_DIM // 2, 1),
                            pltpu.roll(t, ROT_DIM // 2, 1))
        return t * cos + swapped * sin

    qa = proj(A_WIDTH)
    for hd in range(MOBA_HEADS):
        sl = slice(hd * HEAD_DIM, (hd + 1) * HEAD_DIM)
        qh = rope(qa[:, sl]) * QK_LOG2_SCALE
        for r in range(rows // MOBA_BLOCK):
            rs = slice(r * MOBA_BLOCK, (r + 1) * MOBA_BLOCK)
            qt_ref[r, sl, :] = qh[rs].T.astype(_BF16)
    ka = proj(A_WIDTH)
    for hd in range(MOBA_HEADS):
        sl = slice(hd * HEAD_DIM, (hd + 1) * HEAD_DIM)
        kh = rope(ka[:, sl])
        k_ref[:, sl] = kh.astype(_BF16)
        kmean_ref[:, sl] = jnp.mean(
            kh.reshape(rows // MOBA_BLOCK, MOBA_BLOCK, HEAD_DIM), axis=1)
    va = proj(A_WIDTH)
    for r in range(rows // MOBA_BLOCK):
        rs = slice(r * MOBA_BLOCK, (r + 1) * MOBA_BLOCK)
        vt_ref[r] = va[rs].T.astype(_BF16)
    sga_ref[...] = _silu(proj(A_WIDTH)).astype(_BF16)

    ub = proj(B_WIDTH)
    vb = proj(B_WIDTH)
    gb = proj(B_WIDTH)
    mu = jnp.mean(vb, axis=-1, keepdims=True)
    var = jnp.mean(jnp.square(vb - mu), axis=-1, keepdims=True)
    vn = ((vb - mu) * lax.rsqrt(var + EPS) * lng_ref[...]).astype(_BF16)
    gate_b = ub * _silu(gb)
    r_i = lax.broadcasted_iota(jnp.int32, (GMLP_ROWS, GMLP_ROWS), 0)
    c_i = lax.broadcasted_iota(jnp.int32, (GMLP_ROWS, GMLP_ROWS), 1)
    causal = r_i >= c_i
    bsp = bsp_ref[...]
    for g in range(GMLP_GROUPS):
        sl = slice(g * HEAD_DIM, (g + 1) * HEAD_DIM)
        wg = jnp.where(causal, wsp_ref[g], 0.0).astype(_BF16)
        bias = bsp[:, g:g + 1]
        for r in range(rows // GMLP_ROWS):
            rs = slice(r * GMLP_ROWS, (r + 1) * GMLP_ROWS)
            mixed = _dot(wg, vn[rs, sl]) + bias
            outb_scr[rs, sl] = (gate_b[rs, sl] * mixed).astype(_BF16)

    qc = proj(C_WIDTH).astype(_BF16)
    gc = proj(C_WIDTH)
    scale = HEAD_DIM ** -0.5
    for hd in range(XATTN_HEADS):
        sl = slice(hd * HEAD_DIM, (hd + 1) * HEAD_DIM)
        s = _dot_nt(qc[:, sl], mk_ref[:, sl]) * scale
        m = jnp.max(s, axis=-1, keepdims=True)
        p = jnp.exp(s - m)
        l = jnp.sum(p, axis=-1, keepdims=True)
        o = _dot(p.astype(_BF16), mv_ref[:, sl]) / l
        outc_scr[:, sl] = (o * _silu(gc[:, sl])).astype(_BF16)

    sma_ref[...] = jax.nn.sigmoid(proj(d_model)).astype(_BF16)
    yb = jax.nn.sigmoid(proj(d_model)) * _dot(outb_scr[...], wb_ref[...])
    yc = jax.nn.sigmoid(proj(d_model)) * _dot(outc_scr[...], wc_ref[...])
    ybc_ref[...] = (yb + yc).astype(_BF16)


def _moba_out_kernel(qt_ref, k_ref, vt_ref, kmean_ref, sga_ref, ybc_ref, sma_ref, x_ref,
                     wa_ref, wo_ref, fg_ref, o_ref,
                     qaug_scr, m_scr, acc_scr, outa_scr):
    t = pl.program_id(0)
    n_tiles = pl.num_programs(0) - 1
    n_blocks = kmean_ref.shape[0]
    own = jnp.where(t < n_tiles, t % n_blocks, 0)
    tq = qt_ref.shape[1]

    @pl.when(t == 0)
    def _():
        outa_scr[...] = jnp.zeros(outa_scr.shape, _BF16)
    blk = lax.broadcasted_iota(jnp.int32, (n_blocks, tq), 0)
    valid = blk < own
    key_i = lax.broadcasted_iota(jnp.int32, (MOBA_BLOCK, HEAD_DIM), 0)
    qry_i = lax.broadcasted_iota(jnp.int32, (MOBA_BLOCK, HEAD_DIM), 1)

    for hd in range(MOBA_HEADS):
        sl = slice(hd * HEAD_DIM, (hd + 1) * HEAD_DIM)
        qt = qt_ref[sl, :]
        m_scr[hd] = jnp.full((1, tq), _NEG_INF, _F32)
        acc_scr[hd] = jnp.zeros((HEAD_DIM + SUM_ROWS, tq), _F32)

        gs = _dot(kmean_ref[:, sl].astype(_BF16), qt)
        gs = jnp.where(valid, gs, _NEG_INF)
        sel = jnp.zeros((n_blocks, tq), dtype=jnp.bool_)
        for _ in range(MOBA_TOPK):
            top = jnp.max(gs, axis=0, keepdims=True)
            idx = jnp.min(jnp.where(gs == top, blk, n_blocks), axis=0, keepdims=True)
            pick = blk == idx
            sel = jnp.logical_or(sel, pick)
            gs = jnp.where(pick, _NEG_INF, gs)
        bias = jnp.where(jnp.logical_and(sel, valid), 0.0, MASK_BIAS)
        qaug_scr[hd, 0:HEAD_DIM, :] = qt
        qaug_scr[hd, HEAD_DIM:HEAD_DIM + n_blocks, :] = bias.astype(_BF16)
        qaug_scr[hd, HEAD_DIM + n_blocks:AUG_DIM, :] = jnp.zeros(
            (AUG_DIM - HEAD_DIM - n_blocks, tq), _BF16)

    ya = _dot(outa_scr[...], wa_ref[...])
    y = sma_ref[...].astype(_F32) * ya + ybc_ref[...].astype(_F32)
    z = x_ref[...] + _dot(y.astype(_BF16), wo_ref[...])
    o_ref[...] = _rms_norm(z, fg_ref[...])

    lane = lax.broadcasted_iota(jnp.int32, (MOBA_BLOCK, HEAD_DIM), 1)
    ones_rows = jnp.ones((SUM_ROWS, MOBA_BLOCK), _BF16)

    def scores(item):
        j, hd, onehot, _ = item
        sl = slice(hd * HEAD_DIM, (hd + 1) * HEAD_DIM)
        start = pl.multiple_of(j * MOBA_BLOCK, MOBA_BLOCK)
        kaug = jnp.concatenate([k_ref[pl.ds(start, MOBA_BLOCK), sl], onehot], axis=1)
        return _dot(kaug, qaug_scr[hd])

    def consume(item, s):
        j, hd, _, causal = item
        sl = slice(hd * HEAD_DIM, (hd + 1) * HEAD_DIM)
        ps, alphas = [], []
        for half in range(tq // HEAD_DIM):
            cs = slice(half * HEAD_DIM, (half + 1) * HEAD_DIM)
            sh = s[:, cs]
            if causal:
                sh = jnp.where(key_i <= qry_i + half * HEAD_DIM, sh, _NEG_INF)
            m_old = m_scr[hd, :, cs]
            m_new = jnp.maximum(m_old, jnp.max(sh, axis=0, keepdims=True))
            alphas.append(jnp.exp2(m_old - m_new))
            ps.append(jnp.exp2(sh - m_new).astype(_BF16))
            m_scr[hd, :, cs] = m_new
        vt_aug = jnp.concatenate([vt_ref[j, sl, :], ones_rows], axis=0)
        pv = _dot(vt_aug, jnp.concatenate(ps, axis=1))
        for half, alpha in enumerate(alphas):
            cs = slice(half * HEAD_DIM, (half + 1) * HEAD_DIM)
            acc_scr[hd, :, cs] = alpha * acc_scr[hd, :, cs] + pv[:, cs]

    def run_items(items):
        pending = [scores(item) for item in items[:SCORE_LOOKAHEAD]]
        for i, item in enumerate(items):
            if i + SCORE_LOOKAHEAD < len(items):
                pending.append(scores(items[i + SCORE_LOOKAHEAD]))
            consume(item, pending.pop(0))

    def run_blocks(j0, count):
        items = []
        for u in range(count):
            onehot = jnp.where(lane == j0 + u, 1.0, 0.0).astype(_BF16)
            items += [(j0 + u, hd, onehot, False) for hd in range(MOBA_HEADS)]
        run_items(items)

    no_block = jnp.zeros((MOBA_BLOCK, HEAD_DIM), _BF16)
    run_items([(own, hd, no_block, True) for hd in range(MOBA_HEADS)])

    def trip_body(g, carry):
        run_blocks(BLOCKS_PER_TRIP * g, BLOCKS_PER_TRIP)
        return carry

    lax.fori_loop(0, own // BLOCKS_PER_TRIP, trip_body, 0)
    done = (own // BLOCKS_PER_TRIP) * BLOCKS_PER_TRIP
    width = BLOCKS_PER_TRIP // 2
    while width >= 1:
        @pl.when((own & width) != 0)
        def _(done=done, width=width):
            run_blocks(done, width)
        done = done + (own & width)
        width //= 2

    for hd in range(MOBA_HEADS):
        sl = slice(hd * HEAD_DIM, (hd + 1) * HEAD_DIM)
        o = (acc_scr[hd, :HEAD_DIM, :] / acc_scr[hd, HEAD_DIM:HEAD_DIM + 1, :]).T
        outa_scr[:, sl] = (o * sga_ref[:, sl].astype(_F32)).astype(_BF16)


def _rope_tables(seq):
    inv_freq = ROPE_THETA ** (-jnp.arange(0, ROT_DIM, 2, dtype=_F32) / ROT_DIM)
    ang = jnp.arange(seq).astype(_F32)[:, None] * inv_freq[None, :]
    cos, sin = jnp.cos(ang), jnp.sin(ang)
    rest = HEAD_DIM - ROT_DIM
    cos_t = jnp.concatenate([cos, cos, jnp.ones((seq, rest), _F32)], axis=1)
    sin_t = jnp.concatenate([-sin, sin, jnp.zeros((seq, rest), _F32)], axis=1)
    return cos_t, sin_t


def _const_spec(shape):
    nd = len(shape)
    return pl.BlockSpec(shape, lambda *_: (0,) * nd, pipeline_mode=pl.Buffered(1))


def kernel(x, mem, norm_g, mem_norm_g, final_norm_g, w_in, w_mem_kv, gmlp_ln_g,
           w_spatial, b_spatial, w_branch_a, w_branch_b, w_branch_c, w_out):
    batch, seq, d_model = x.shape
    mem_len = mem.shape[1]
    assert w_in.shape[0] == 1, "single layer"
    assert seq % PROJ_ROWS == 0 and PROJ_ROWS % MOBA_BLOCK == 0 and PROJ_ROWS % GMLP_ROWS == 0
    n_tok = batch * seq
    n_blocks = seq // MOBA_BLOCK
    assert HEAD_DIM + n_blocks <= AUG_DIM and n_blocks % 16 == 0
    in_width = w_in.shape[-1]

    x2 = x.reshape(n_tok, d_model)
    w_in_b = w_in[0].astype(_BF16)
    cos_t, sin_t = _rope_tables(seq)
    eye = jnp.eye(GMLP_ROWS // GMLP_CHUNK, dtype=_F32)
    wsp_bd = jnp.einsum("ab,gts->gatbs", eye, w_spatial[0]).reshape(
        GMLP_GROUPS, GMLP_ROWS, GMLP_ROWS)
    bsp_t = jnp.tile(jnp.transpose(b_spatial[0]), (GMLP_ROWS // GMLP_CHUNK, 1))

    params = functools.partial(pltpu.CompilerParams, vmem_limit_bytes=V7X_VMEM_LIMIT_BYTES)

    mk, mv = pl.pallas_call(
        _mem_kv_kernel,
        grid=(batch,),
        in_specs=[
            pl.BlockSpec((None, mem_len, d_model), lambda b: (b, 0, 0)),
            pl.BlockSpec((1, d_model), lambda b: (0, 0)),
            pl.BlockSpec((d_model, 2 * C_WIDTH), lambda b: (0, 0)),
        ],
        out_specs=[
            pl.BlockSpec((None, mem_len, C_WIDTH), lambda b: (b, 0, 0)),
            pl.BlockSpec((None, mem_len, C_WIDTH), lambda b: (b, 0, 0)),
        ],
        out_shape=[jax.ShapeDtypeStruct((batch, mem_len, C_WIDTH), _BF16)] * 2,
        compiler_params=params(dimension_semantics=("arbitrary",)),
        name="mem_kv",
    )(mem, mem_norm_g[0].reshape(1, d_model), w_mem_kv[0].astype(_BF16))

    tiles_per_seq = seq // PROJ_ROWS
    blocks_per_tile = PROJ_ROWS // MOBA_BLOCK
    row_spec = lambda w: pl.BlockSpec((PROJ_ROWS, w), lambda i: (i, 0))
    blk_t_spec = pl.BlockSpec((blocks_per_tile, A_WIDTH, MOBA_BLOCK), lambda i: (i, 0, 0))
    pos_spec = pl.BlockSpec((PROJ_ROWS, HEAD_DIM), lambda i: (i % tiles_per_seq, 0))
    mem_spec = pl.BlockSpec((None, mem_len, C_WIDTH), lambda i: (i // tiles_per_seq, 0, 0))
    blk_t_shape = jax.ShapeDtypeStruct((n_tok // MOBA_BLOCK, A_WIDTH, MOBA_BLOCK), _BF16)
    qt, ka, vt, sga, kmean, ybc, sma = pl.pallas_call(
        _proj_kernel,
        grid=(n_tok // PROJ_ROWS,),
        in_specs=[
            row_spec(d_model),
            _const_spec((1, d_model)),
            _const_spec((d_model, in_width)),
            pos_spec, pos_spec,
            _const_spec((1, B_WIDTH)),
            _const_spec((GMLP_GROUPS, GMLP_ROWS, GMLP_ROWS)),
            _const_spec((GMLP_ROWS, GMLP_GROUPS)),
            mem_spec, mem_spec,
            _const_spec((B_WIDTH, d_model)),
            _const_spec((C_WIDTH, d_model)),
        ],
        out_specs=[
            blk_t_spec, row_spec(A_WIDTH), blk_t_spec, row_spec(A_WIDTH),
            pl.BlockSpec((None, blocks_per_tile, A_WIDTH), lambda i: (i, 0, 0)),
            row_spec(d_model), row_spec(d_model),
        ],
        out_shape=[
            blk_t_shape,
            jax.ShapeDtypeStruct((n_tok, A_WIDTH), _BF16),
            blk_t_shape,
            jax.ShapeDtypeStruct((n_tok, A_WIDTH), _BF16),
            jax.ShapeDtypeStruct((n_tok // PROJ_ROWS, blocks_per_tile, A_WIDTH), _F32),
            jax.ShapeDtypeStruct((n_tok, d_model), _BF16),
            jax.ShapeDtypeStruct((n_tok, d_model), _BF16),
        ],
        scratch_shapes=[
            pltpu.VMEM((PROJ_ROWS, B_WIDTH), _BF16),
            pltpu.VMEM((PROJ_ROWS, C_WIDTH), _BF16),
        ],
        compiler_params=params(dimension_semantics=("arbitrary",)),
        name="proj_branches",
    )(x2, norm_g[0].reshape(1, d_model), w_in_b, cos_t, sin_t,
      gmlp_ln_g[0].reshape(1, B_WIDTH), wsp_bd, bsp_t, mk, mv,
      w_branch_b[0].astype(_BF16), w_branch_c[0].astype(_BF16))

    kmean = kmean.reshape(batch, n_blocks, A_WIDTH)
    k3 = ka.reshape(batch, seq, A_WIDTH)
    vt4 = vt.reshape(batch, n_blocks, A_WIDTH, MOBA_BLOCK)
    n_tiles = batch * n_blocks
    attn_tile = lambda t: jnp.minimum(t, n_tiles - 1)
    done_tile = lambda t: jnp.maximum(t - 1, 0)
    attn_spec = lambda w: pl.BlockSpec((MOBA_BLOCK, w), lambda t: (attn_tile(t), 0))
    done_spec = lambda w: pl.BlockSpec((MOBA_BLOCK, w), lambda t: (done_tile(t), 0))
    out = pl.pallas_call(
        _moba_out_kernel,
        grid=(n_tiles + 1,),
        in_specs=[
            pl.BlockSpec((None, A_WIDTH, MOBA_BLOCK), lambda t: (attn_tile(t), 0, 0)),
            pl.BlockSpec((None, seq, A_WIDTH), lambda t: (attn_tile(t) // n_blocks, 0, 0),
                         pipeline_mode=pl.Buffered(1)),
            pl.BlockSpec((None, n_blocks, A_WIDTH, MOBA_BLOCK),
                         lambda t: (attn_tile(t) // n_blocks, 0, 0, 0),
                         pipeline_mode=pl.Buffered(1)),
            pl.BlockSpec((None, n_blocks, A_WIDTH), lambda t: (attn_tile(t) // n_blocks, 0, 0)),
            attn_spec(A_WIDTH), done_spec(d_model), done_spec(d_model), done_spec(d_model),
            _const_spec((A_WIDTH, d_model)),
            _const_spec((d_model, d_model)),
            _const_spec((1, d_model)),
        ],
        out_specs=done_spec(d_model),
        out_shape=jax.ShapeDtypeStruct((n_tok, d_model), _F32),
        scratch_shapes=[
            pltpu.VMEM((MOBA_HEADS, AUG_DIM, MOBA_BLOCK), _BF16),
            pltpu.VMEM((MOBA_HEADS, 1, MOBA_BLOCK), _F32),
            pltpu.VMEM((MOBA_HEADS, HEAD_DIM + SUM_ROWS, MOBA_BLOCK), _F32),
            pltpu.VMEM((MOBA_BLOCK, A_WIDTH), _BF16),
        ],
        compiler_params=params(dimension_semantics=("arbitrary",)),
        name="moba_merge_out",
    )(qt, k3, vt4, kmean, sga, ybc, sma, x2,
      w_branch_a[0].astype(_BF16), w_out[0].astype(_BF16), final_norm_g.reshape(1, d_model))
    return out.reshape(batch, seq, d_model)
```

```python
import functools
import math

import jax
import jax.numpy as jnp
import numpy as np
from jax import lax
from jax.experimental import pallas as pl
from jax.experimental.pallas import tpu as pltpu

HEAD_DIM = 128
MOBA_HEADS = 6
MOBA_BLOCK = 256
MOBA_TOPK = 3
GMLP_GROUPS = 6
GMLP_CHUNK = 128
XATTN_HEADS = 4
ROPE_THETA = 500000.0
ROT_DIM = HEAD_DIM // 4
EPS = 1e-6

A_WIDTH = MOBA_HEADS * HEAD_DIM
B_WIDTH = GMLP_GROUPS * HEAD_DIM
C_WIDTH = XATTN_HEADS * HEAD_DIM

V7X_VMEM_LIMIT_BYTES = 58 * 1024 * 1024

PROJ_ROWS = 512
GMLP_ROWS = 2 * GMLP_CHUNK

QK_LOG2_SCALE = HEAD_DIM ** -0.5 * math.log2(math.e)
MASK_BIAS = -1e30
AUG_DIM = 2 * HEAD_DIM

_BF16 = jnp.bfloat16
_F32 = jnp.float32
_NEG_INF = float("-inf")


def _rms_norm(x, g):
    return x * lax.rsqrt(jnp.mean(x * x, axis=-1, keepdims=True) + EPS) * g


def _silu(x):
    return x * jax.nn.sigmoid(x)


def _dot(a, b):
    return jnp.dot(a, b, preferred_element_type=_F32)


def _dot_nt(a, b):
    return lax.dot_general(a, b, (((1,), (1,)), ((), ())), preferred_element_type=_F32)


def _mem_kv_kernel(mem_ref, g_ref, w_ref, mk_ref, mv_ref):
    mn = _rms_norm(mem_ref[...], g_ref[...]).astype(_BF16)
    kv = _dot(mn, w_ref[...])
    mk_ref[...] = kv[:, :C_WIDTH].astype(_BF16)
    mv_ref[...] = kv[:, C_WIDTH:].astype(_BF16)


def _proj_kernel(x_ref, g_ref, w_ref, cos_ref, sin_ref, lng_ref, wsp_ref, bsp_ref,
                 mk_ref, mv_ref, wb_ref, wc_ref,
                 qt_ref, k_ref, vt_ref, sga_ref, kmean_ref, ybc_ref, sma_ref,
                 outb_scr, outc_scr):
    rows = x_ref.shape[0]
    d_model = x_ref.shape[1]
    h = _rms_norm(x_ref[...], g_ref[...]).astype(_BF16)

    off = [0]

    def proj(width):
        a = off[0]
        off[0] = a + width
        return _dot(h, w_ref[:, a:a + width])

    cos = cos_ref[...]
    sin = sin_ref[...]
    lane = lax.broadcasted_iota(jnp.int32, (rows, HEAD_DIM), 1)
    first_half = lane < (ROT_DIM // 2)

    def rope(t):
        swapped = jnp.where(first_half,
                            pltpu.roll(t, HEAD_DIM - ROT_DIM // 2, 1),
                            pltpu.roll(t, ROT_DIM // 2, 1))
        return t * cos + swapped * sin

    qa = proj(A_WIDTH)
    for hd in range(MOBA_HEADS):
        sl = slice(hd * HEAD_DIM, (hd + 1) * HEAD_DIM)
        qh = rope(qa[:, sl]) * QK_LOG2_SCALE
        for r in range(rows // MOBA_BLOCK):
            rs = slice(r * MOBA_BLOCK, (r + 1) * MOBA_BLOCK)
            qt_ref[r, sl, :] = qh[rs].T.astype(_BF16)
    ka = proj(A_WIDTH)
    for hd in range(MOBA_HEADS):
        sl = slice(hd * HEAD_DIM, (hd + 1) * HEAD_DIM)
        kh = rope(ka[:, sl])
        k_ref[:, sl] = kh.astype(_BF16)
        kmean_ref[:, sl] = jnp.mean(
            kh.reshape(rows // MOBA_BLOCK, MOBA_BLOCK, HEAD_DIM), axis=1)
    va = proj(A_WIDTH)
    for r in range(rows // MOBA_BLOCK):
        rs = slice(r * MOBA_BLOCK, (r + 1) * MOBA_BLOCK)
        vt_ref[r] = va[rs].T.astype(_BF16)
    sga_ref[...] = _silu(proj(A_WIDTH)).astype(_BF16)

    ub = proj(B_WIDTH)
    vb = proj(B_WIDTH)
    gb = proj(B_WIDTH)
    mu = jnp.mean(vb, axis=-1, keepdims=True)
    var = jnp.mean(jnp.square(vb - mu), axis=-1, keepdims=True)
    vn = ((vb - mu) * lax.rsqrt(var + EPS) * lng_ref[...]).astype(_BF16)
    gate_b = ub * _silu(gb)
    r_i = lax.broadcasted_iota(jnp.int32, (GMLP_ROWS, GMLP_ROWS), 0)
    c_i = lax.broadcasted_iota(jnp.int32, (GMLP_ROWS, GMLP_ROWS), 1)
    causal = r_i >= c_i
    bsp = bsp_ref[...]
    for g in range(GMLP_GROUPS):
        sl = slice(g * HEAD_DIM, (g + 1) * HEAD_DIM)
        wg = jnp.where(causal, wsp_ref[g], 0.0).astype(_BF16)
        bias = bsp[:, g:g + 1]
        for r in range(rows // GMLP_ROWS):
            rs = slice(r * GMLP_ROWS, (r + 1) * GMLP_ROWS)
            mixed = _dot(wg, vn[rs, sl]) + bias
            outb_scr[rs, sl] = (gate_b[rs, sl] * mixed).astype(_BF16)

    qc = proj(C_WIDTH).astype(_BF16)
    gc = proj(C_WIDTH)
    scale = HEAD_DIM ** -0.5
    for hd in range(XATTN_HEADS):
        sl = slice(hd * HEAD_DIM, (hd + 1) * HEAD_DIM)
        s = _dot_nt(qc[:, sl], mk_ref[:, sl]) * scale
        m = jnp.max(s, axis=-1, keepdims=True)
        p = jnp.exp(s - m)
        l = jnp.sum(p, axis=-1, keepdims=True)
        o = _dot(p.astype(_BF16), mv_ref[:, sl]) / l
        outc_scr[:, sl] = (o * _silu(gc[:, sl])).astype(_BF16)

    sma_ref[...] = jax.nn.sigmoid(proj(d_model)).astype(_BF16)
    yb = jax.nn.sigmoid(proj(d_model)) * _dot(outb_scr[...], wb_ref[...])
    yc = jax.nn.sigmoid(proj(d_model)) * _dot(outc_scr[...], wc_ref[...])
    ybc_ref[...] = (yb + yc).astype(_BF16)


def _moba_out_kernel(qt_ref, k_ref, vt_ref, kmean_ref, sga_ref, ybc_ref, sma_ref, x_ref,
                     wa_ref, wo_ref, fg_ref, o_ref,
                     qaug_scr, sa_scr, sb_scr, sd_scr, m_scr, l_scr, acc_scr, outa_scr):
    own = pl.program_id(1)
    n_blocks = kmean_ref.shape[0]
    tq = qt_ref.shape[1]
    blk = lax.broadcasted_iota(jnp.int32, (n_blocks, tq), 0)
    valid = blk < own
    key_i = lax.broadcasted_iota(jnp.int32, (MOBA_BLOCK, HEAD_DIM), 0)
    qry_i = lax.broadcasted_iota(jnp.int32, (MOBA_BLOCK, HEAD_DIM), 1)
    own_start = pl.multiple_of(own * MOBA_BLOCK, MOBA_BLOCK)

    for hd in range(MOBA_HEADS):
        sl = slice(hd * HEAD_DIM, (hd + 1) * HEAD_DIM)
        qt = qt_ref[sl, :]

        gs = _dot(kmean_ref[:, sl].astype(_BF16), qt)
        gs = jnp.where(valid, gs, _NEG_INF)
        sel = jnp.zeros((n_blocks, tq), dtype=jnp.bool_)
        for _ in range(MOBA_TOPK):
            top = jnp.max(gs, axis=0, keepdims=True)
            idx = jnp.min(jnp.where(gs == top, blk, n_blocks), axis=0, keepdims=True)
            pick = blk == idx
            sel = jnp.logical_or(sel, pick)
            gs = jnp.where(pick, _NEG_INF, gs)
        bias = jnp.where(jnp.logical_and(sel, valid), 0.0, MASK_BIAS)
        qaug_scr[hd, 0:HEAD_DIM, :] = qt
        qaug_scr[hd, HEAD_DIM:HEAD_DIM + n_blocks, :] = bias.astype(_BF16)
        qaug_scr[hd, HEAD_DIM + n_blocks:AUG_DIM, :] = jnp.zeros(
            (AUG_DIM - HEAD_DIM - n_blocks, tq), _BF16)

        sd_scr[hd] = _dot(k_ref[pl.ds(own_start, MOBA_BLOCK), sl], qt)
        m_scr[hd] = jnp.full((1, tq), _NEG_INF, _F32)
        l_scr[hd] = jnp.zeros((1, tq), _F32)
        acc_scr[hd] = jnp.zeros((HEAD_DIM, tq), _F32)

    lane = lax.broadcasted_iota(jnp.int32, (MOBA_BLOCK, HEAD_DIM), 1)

    def masked_scores(j, hd):
        sl = slice(hd * HEAD_DIM, (hd + 1) * HEAD_DIM)
        start = pl.multiple_of(j * MOBA_BLOCK, MOBA_BLOCK)
        onehot = jnp.where(lane == j, 1.0, 0.0).astype(_BF16)
        kaug = jnp.concatenate([k_ref[pl.ds(start, MOBA_BLOCK), sl], onehot], axis=1)
        return _dot(kaug, qaug_scr[hd])

    def softmax_pv(j, hd, s_ref, causal=False):
        sl = slice(hd * HEAD_DIM, (hd + 1) * HEAD_DIM)
        ps, alphas = [], []
        for half in range(tq // HEAD_DIM):
            cs = slice(half * HEAD_DIM, (half + 1) * HEAD_DIM)
            s = s_ref[hd, :, cs]
            if causal:
                s = jnp.where(key_i <= qry_i + half * HEAD_DIM, s, _NEG_INF)
            m_old = m_scr[hd, :, cs]
            m_new = jnp.maximum(m_old, jnp.max(s, axis=0, keepdims=True))
            alpha = jnp.exp2(m_old - m_new)
            p = jnp.exp2(s - m_new)
            m_scr[hd, :, cs] = m_new
            l_scr[hd, :, cs] = alpha * l_scr[hd, :, cs] + jnp.sum(p, axis=0, keepdims=True)
            ps.append(p.astype(_BF16))
            alphas.append(alpha)
        pv = _dot(vt_ref[j, sl, :], jnp.concatenate(ps, axis=1))
        for half, alpha in enumerate(alphas):
            cs = slice(half * HEAD_DIM, (half + 1) * HEAD_DIM)
            acc_scr[hd, :, cs] = alpha * acc_scr[hd, :, cs] + pv[:, cs]

    for hd in range(MOBA_HEADS):
        sa_scr[hd] = masked_scores(0, hd)
        sb_scr[hd] = masked_scores(1, hd)

    def pair_body(jj, carry):
        j0 = 2 * jj
        for hd in range(MOBA_HEADS):
            s_next = masked_scores(j0 + 2, hd)
            softmax_pv(j0, hd, sa_scr)
            sa_scr[hd] = s_next
        j3 = jnp.minimum(j0 + 3, n_blocks - 1)
        for hd in range(MOBA_HEADS):
            s_next = masked_scores(j3, hd)
            softmax_pv(j0 + 1, hd, sb_scr)
            sb_scr[hd] = s_next
        return carry

    lax.fori_loop(0, own // 2, pair_body, 0)

    @pl.when(own % 2 == 1)
    def _():
        for hd in range(MOBA_HEADS):
            softmax_pv(own - 1, hd, sa_scr)

    for hd in range(MOBA_HEADS):
        softmax_pv(own, hd, sd_scr, causal=True)

    for hd in range(MOBA_HEADS):
        sl = slice(hd * HEAD_DIM, (hd + 1) * HEAD_DIM)
        o = (acc_scr[hd] / l_scr[hd]).T
        outa_scr[:, sl] = (o * sga_ref[:, sl].astype(_F32)).astype(_BF16)

    ya = _dot(outa_scr[...], wa_ref[...])
    y = sma_ref[...].astype(_F32) * ya + ybc_ref[...].astype(_F32)
    z = x_ref[...] + _dot(y.astype(_BF16), wo_ref[...])
    o_ref[...] = _rms_norm(z, fg_ref[...])


def _rope_tables(seq):
    inv_freq = np.float32(ROPE_THETA) ** (-np.arange(0, ROT_DIM, 2, dtype=np.float32) / ROT_DIM)
    ang = np.arange(seq).astype(np.float32)[:, None] * inv_freq[None, :].astype(np.float32)
    cos, sin = np.cos(ang), np.sin(ang)
    rest = HEAD_DIM - ROT_DIM
    cos_t = np.concatenate([cos, cos, np.ones((seq, rest), np.float32)], axis=1)
    sin_t = np.concatenate([-sin, sin, np.zeros((seq, rest), np.float32)], axis=1)
    return jnp.asarray(cos_t, _F32), jnp.asarray(sin_t, _F32)


def _const_spec(shape):
    nd = len(shape)
    return pl.BlockSpec(shape, lambda *_: (0,) * nd, pipeline_mode=pl.Buffered(1))


def kernel(x, mem, norm_g, mem_norm_g, final_norm_g, w_in, w_mem_kv, gmlp_ln_g,
           w_spatial, b_spatial, w_branch_a, w_branch_b, w_branch_c, w_out):
    batch, seq, d_model = x.shape
    mem_len = mem.shape[1]
    assert w_in.shape[0] == 1, "single layer"
    assert seq % PROJ_ROWS == 0 and PROJ_ROWS % MOBA_BLOCK == 0 and PROJ_ROWS % GMLP_ROWS == 0
    n_tok = batch * seq
    n_blocks = seq // MOBA_BLOCK
    assert HEAD_DIM + n_blocks <= AUG_DIM and n_blocks % 16 == 0
    in_width = w_in.shape[-1]

    x2 = x.reshape(n_tok, d_model)
    w_in_b = w_in[0].astype(_BF16)
    cos_t, sin_t = _rope_tables(seq)
    eye = jnp.eye(GMLP_ROWS // GMLP_CHUNK, dtype=_F32)
    wsp_bd = jnp.einsum("ab,gts->gatbs", eye, w_spatial[0]).reshape(
        GMLP_GROUPS, GMLP_ROWS, GMLP_ROWS)
    bsp_t = jnp.tile(jnp.transpose(b_spatial[0]), (GMLP_ROWS // GMLP_CHUNK, 1))

    params = functools.partial(pltpu.CompilerParams, vmem_limit_bytes=V7X_VMEM_LIMIT_BYTES)

    mk, mv = pl.pallas_call(
        _mem_kv_kernel,
        grid=(batch,),
        in_specs=[
            pl.BlockSpec((None, mem_len, d_model), lambda b: (b, 0, 0)),
            pl.BlockSpec((1, d_model), lambda b: (0, 0)),
            pl.BlockSpec((d_model, 2 * C_WIDTH), lambda b: (0, 0)),
        ],
        out_specs=[
            pl.BlockSpec((None, mem_len, C_WIDTH), lambda b: (b, 0, 0)),
            pl.BlockSpec((None, mem_len, C_WIDTH), lambda b: (b, 0, 0)),
        ],
        out_shape=[jax.ShapeDtypeStruct((batch, mem_len, C_WIDTH), _BF16)] * 2,
        compiler_params=params(dimension_semantics=("arbitrary",)),
        name="mem_kv",
    )(mem, mem_norm_g[0].reshape(1, d_model), w_mem_kv[0].astype(_BF16))

    tiles_per_seq = seq // PROJ_ROWS
    blocks_per_tile = PROJ_ROWS // MOBA_BLOCK
    row_spec = lambda w: pl.BlockSpec((PROJ_ROWS, w), lambda i: (i, 0))
    blk_t_spec = pl.BlockSpec((blocks_per_tile, A_WIDTH, MOBA_BLOCK), lambda i: (i, 0, 0))
    pos_spec = pl.BlockSpec((PROJ_ROWS, HEAD_DIM), lambda i: (i % tiles_per_seq, 0))
    mem_spec = pl.BlockSpec((None, mem_len, C_WIDTH), lambda i: (i // tiles_per_seq, 0, 0))
    blk_t_shape = jax.ShapeDtypeStruct((n_tok // MOBA_BLOCK, A_WIDTH, MOBA_BLOCK), _BF16)
    qt, ka, vt, sga, kmean, ybc, sma = pl.pallas_call(
        _proj_kernel,
        grid=(n_tok // PROJ_ROWS,),
        in_specs=[
            row_spec(d_model),
            _const_spec((1, d_model)),
            _const_spec((d_model, in_width)),
            pos_spec, pos_spec,
            _const_spec((1, B_WIDTH)),
            _const_spec((GMLP_GROUPS, GMLP_ROWS, GMLP_ROWS)),
            _const_spec((GMLP_ROWS, GMLP_GROUPS)),
            mem_spec, mem_spec,
            _const_spec((B_WIDTH, d_model)),
            _const_spec((C_WIDTH, d_model)),
        ],
        out_specs=[
            blk_t_spec, row_spec(A_WIDTH), blk_t_spec, row_spec(A_WIDTH),
            pl.BlockSpec((None, blocks_per_tile, A_WIDTH), lambda i: (i, 0, 0)),
            row_spec(d_model), row_spec(d_model),
        ],
        out_shape=[
            blk_t_shape,
            jax.ShapeDtypeStruct((n_tok, A_WIDTH), _BF16),
            blk_t_shape,
            jax.ShapeDtypeStruct((n_tok, A_WIDTH), _BF16),
            jax.ShapeDtypeStruct((n_tok // PROJ_ROWS, blocks_per_tile, A_WIDTH), _F32),
            jax.ShapeDtypeStruct((n_tok, d_model), _BF16),
            jax.ShapeDtypeStruct((n_tok, d_model), _BF16),
        ],
        scratch_shapes=[
            pltpu.VMEM((PROJ_ROWS, B_WIDTH), _BF16),
            pltpu.VMEM((PROJ_ROWS, C_WIDTH), _BF16),
        ],
        compiler_params=params(dimension_semantics=("arbitrary",)),
        name="proj_branches",
    )(x2, norm_g[0].reshape(1, d_model), w_in_b, cos_t, sin_t,
      gmlp_ln_g[0].reshape(1, B_WIDTH), wsp_bd, bsp_t, mk, mv,
      w_branch_b[0].astype(_BF16), w_branch_c[0].astype(_BF16))

    kmean = kmean.reshape(batch, n_blocks, A_WIDTH)
    k3 = ka.reshape(batch, seq, A_WIDTH)
    vt4 = vt.reshape(batch, n_blocks, A_WIDTH, MOBA_BLOCK)
    tile_spec = lambda w: pl.BlockSpec((MOBA_BLOCK, w), lambda b, i: (b * n_blocks + i, 0))
    out = pl.pallas_call(
        _moba_out_kernel,
        grid=(batch, n_blocks),
        in_specs=[
            pl.BlockSpec((None, A_WIDTH, MOBA_BLOCK), lambda b, i: (b * n_blocks + i, 0, 0)),
            pl.BlockSpec((None, seq, A_WIDTH), lambda b, i: (b, 0, 0),
                         pipeline_mode=pl.Buffered(1)),
            pl.BlockSpec((None, n_blocks, A_WIDTH, MOBA_BLOCK), lambda b, i: (b, 0, 0, 0),
                         pipeline_mode=pl.Buffered(1)),
            pl.BlockSpec((None, n_blocks, A_WIDTH), lambda b, i: (b, 0, 0)),
            tile_spec(A_WIDTH), tile_spec(d_model), tile_spec(d_model), tile_spec(d_model),
            _const_spec((A_WIDTH, d_model)),
            _const_spec((d_model, d_model)),
            _const_spec((1, d_model)),
        ],
        out_specs=tile_spec(d_model),
        out_shape=jax.ShapeDtypeStruct((n_tok, d_model), _F32),
        scratch_shapes=[
            pltpu.VMEM((MOBA_HEADS, AUG_DIM, MOBA_BLOCK), _BF16),
            pltpu.VMEM((MOBA_HEADS, MOBA_BLOCK, MOBA_BLOCK), _F32),
            pltpu.VMEM((MOBA_HEADS, MOBA_BLOCK, MOBA_BLOCK), _F32),
            pltpu.VMEM((MOBA_HEADS, MOBA_BLOCK, MOBA_BLOCK), _F32),
            pltpu.VMEM((MOBA_HEADS, 1, MOBA_BLOCK), _F32),
            pltpu.VMEM((MOBA_HEADS, 1, MOBA_BLOCK), _F32),
            pltpu.VMEM((MOBA_HEADS, HEAD_DIM, MOBA_BLOCK), _F32),
            pltpu.VMEM((MOBA_BLOCK, A_WIDTH), _BF16),
        ],
        compiler_params=params(dimension_semantics=("arbitrary", "arbitrary")),
        name="moba_merge_out",
    )(qt, k3, vt4, kmean, sga, ybc, sma, x2,
      w_branch_a[0].astype(_BF16), w_out[0].astype(_BF16), final_norm_g.reshape(1, d_model))
    return out.reshape(batch, seq, d_model)
```

```python
import functools
import math

import jax
import jax.numpy as jnp
import numpy as np
from jax import lax
from jax.experimental import pallas as pl
from jax.experimental.pallas import tpu as pltpu

HEAD_DIM = 128
MOBA_HEADS = 6
MOBA_BLOCK = 256
MOBA_TOPK = 3
GMLP_GROUPS = 6
GMLP_CHUNK = 128
XATTN_HEADS = 4
ROPE_THETA = 500000.0
ROT_DIM = HEAD_DIM // 4
EPS = 1e-6

A_WIDTH = MOBA_HEADS * HEAD_DIM
B_WIDTH = GMLP_GROUPS * HEAD_DIM
C_WIDTH = XATTN_HEADS * HEAD_DIM

V7X_VMEM_LIMIT_BYTES = 58 * 1024 * 1024

PROJ_ROWS = 512

QK_LOG2_SCALE = HEAD_DIM ** -0.5 * math.log2(math.e)
MASK_BIAS = -1e30
AUG_DIM = 2 * HEAD_DIM

_BF16 = jnp.bfloat16
_F32 = jnp.float32
_NEG_INF = float("-inf")


def _rms_norm(x, g):
    return x * lax.rsqrt(jnp.mean(x * x, axis=-1, keepdims=True) + EPS) * g


def _silu(x):
    return x * jax.nn.sigmoid(x)


def _dot(a, b):
    return jnp.dot(a, b, preferred_element_type=_F32)


def _dot_nt(a, b):
    return lax.dot_general(a, b, (((1,), (1,)), ((), ())), preferred_element_type=_F32)


def _mem_kv_kernel(mem_ref, g_ref, w_ref, mk_ref, mv_ref):
    mn = _rms_norm(mem_ref[...], g_ref[...]).astype(_BF16)
    kv = _dot(mn, w_ref[...])
    mk_ref[...] = kv[:, :C_WIDTH].astype(_BF16)
    mv_ref[...] = kv[:, C_WIDTH:].astype(_BF16)


def _proj_kernel(x_ref, g_ref, w_ref, cos_ref, sin_ref, lng_ref, wsp_ref, bsp_ref,
                 mk_ref, mv_ref, wb_ref, wc_ref,
                 qt_ref, k_ref, vt_ref, sga_ref, kmean_ref, ybc_ref, sma_ref,
                 outb_scr, outc_scr):
    rows = x_ref.shape[0]
    d_model = x_ref.shape[1]
    h = _rms_norm(x_ref[...], g_ref[...]).astype(_BF16)

    off = [0]

    def proj(width):
        a = off[0]
        off[0] = a + width
        return _dot(h, w_ref[:, a:a + width])

    cos = cos_ref[...]
    sin = sin_ref[...]
    lane = lax.broadcasted_iota(jnp.int32, (rows, HEAD_DIM), 1)
    first_half = lane < (ROT_DIM // 2)

    def rope(t):
        swapped = jnp.where(first_half,
                            pltpu.roll(t, HEAD_DIM - ROT_DIM // 2, 1),
                            pltpu.roll(t, ROT_DIM // 2, 1))
        return t * cos + swapped * sin

    qa = proj(A_WIDTH)
    for hd in range(MOBA_HEADS):
        sl = slice(hd * HEAD_DIM, (hd + 1) * HEAD_DIM)
        qh = rope(qa[:, sl]) * QK_LOG2_SCALE
        for r in range(rows // MOBA_BLOCK):
            rs = slice(r * MOBA_BLOCK, (r + 1) * MOBA_BLOCK)
            qt_ref[r, sl, :] = qh[rs].T.astype(_BF16)
    ka = proj(A_WIDTH)
    for hd in range(MOBA_HEADS):
        sl = slice(hd * HEAD_DIM, (hd + 1) * HEAD_DIM)
        kh = rope(ka[:, sl])
        k_ref[:, sl] = kh.astype(_BF16)
        kmean_ref[:, sl] = jnp.mean(
            kh.reshape(rows // MOBA_BLOCK, MOBA_BLOCK, HEAD_DIM), axis=1)
    va = proj(A_WIDTH)
    for r in range(rows // MOBA_BLOCK):
        rs = slice(r * MOBA_BLOCK, (r + 1) * MOBA_BLOCK)
        vt_ref[r] = va[rs].T.astype(_BF16)
    sga_ref[...] = _silu(proj(A_WIDTH)).astype(_BF16)

    ub = proj(B_WIDTH)
    vb = proj(B_WIDTH)
    gb = proj(B_WIDTH)
    mu = jnp.mean(vb, axis=-1, keepdims=True)
    var = jnp.mean(jnp.square(vb - mu), axis=-1, keepdims=True)
    vn = ((vb - mu) * lax.rsqrt(var + EPS) * lng_ref[...]).astype(_BF16)
    gate_b = ub * _silu(gb)
    r_i = lax.broadcasted_iota(jnp.int32, (GMLP_CHUNK, GMLP_CHUNK), 0)
    c_i = lax.broadcasted_iota(jnp.int32, (GMLP_CHUNK, GMLP_CHUNK), 1)
    causal = r_i >= c_i
    bsp = bsp_ref[...]
    chunks = [slice(c * GMLP_CHUNK, (c + 1) * GMLP_CHUNK) for c in range(rows // GMLP_CHUNK)]
    for g in range(GMLP_GROUPS):
        sl = slice(g * HEAD_DIM, (g + 1) * HEAD_DIM)
        wg = jnp.where(causal, wsp_ref[g], 0.0).astype(_BF16)
        mixed = _dot(wg, jnp.concatenate([vn[rs, sl] for rs in chunks], axis=1))
        bias = bsp[:, g:g + 1]
        for c, rs in enumerate(chunks):
            mixed_c = mixed[:, c * HEAD_DIM:(c + 1) * HEAD_DIM] + bias
            outb_scr[rs, sl] = (gate_b[rs, sl] * mixed_c).astype(_BF16)

    qc = proj(C_WIDTH).astype(_BF16)
    gc = proj(C_WIDTH)
    scale = HEAD_DIM ** -0.5
    for hd in range(XATTN_HEADS):
        sl = slice(hd * HEAD_DIM, (hd + 1) * HEAD_DIM)
        s = _dot_nt(qc[:, sl], mk_ref[:, sl]) * scale
        m = jnp.max(s, axis=-1, keepdims=True)
        p = jnp.exp(s - m)
        l = jnp.sum(p, axis=-1, keepdims=True)
        o = _dot(p.astype(_BF16), mv_ref[:, sl]) / l
        outc_scr[:, sl] = (o * _silu(gc[:, sl])).astype(_BF16)

    sma_ref[...] = jax.nn.sigmoid(proj(d_model)).astype(_BF16)
    yb = jax.nn.sigmoid(proj(d_model)) * _dot(outb_scr[...], wb_ref[...])
    yc = jax.nn.sigmoid(proj(d_model)) * _dot(outc_scr[...], wc_ref[...])
    ybc_ref[...] = (yb + yc).astype(_BF16)


def _moba_out_kernel(qt_ref, k_ref, vt_ref, kmean_ref, sga_ref, ybc_ref, sma_ref, x_ref,
                     wa_ref, wo_ref, fg_ref, o_ref,
                     qaug_scr, sa_scr, sb_scr, sd_scr, m_scr, l_scr, acc_scr, outa_scr):
    own = pl.program_id(1)
    n_blocks = kmean_ref.shape[0]
    tq = qt_ref.shape[1]
    blk = lax.broadcasted_iota(jnp.int32, (n_blocks, tq), 0)
    valid = blk < own
    key_i = lax.broadcasted_iota(jnp.int32, (MOBA_BLOCK, HEAD_DIM), 0)
    qry_i = lax.broadcasted_iota(jnp.int32, (MOBA_BLOCK, HEAD_DIM), 1)
    own_start = pl.multiple_of(own * MOBA_BLOCK, MOBA_BLOCK)

    for hd in range(MOBA_HEADS):
        sl = slice(hd * HEAD_DIM, (hd + 1) * HEAD_DIM)
        qt = qt_ref[sl, :]

        gs = _dot(kmean_ref[:, sl].astype(_BF16), qt)
        gs = jnp.where(valid, gs, _NEG_INF)
        sel = jnp.zeros((n_blocks, tq), dtype=jnp.bool_)
        for _ in range(MOBA_TOPK):
            top = jnp.max(gs, axis=0, keepdims=True)
            idx = jnp.min(jnp.where(gs == top, blk, n_blocks), axis=0, keepdims=True)
            pick = blk == idx
            sel = jnp.logical_or(sel, pick)
            gs = jnp.where(pick, _NEG_INF, gs)
        bias = jnp.where(jnp.logical_and(sel, valid), 0.0, MASK_BIAS)
        qaug_scr[hd, 0:HEAD_DIM, :] = qt
        qaug_scr[hd, HEAD_DIM:HEAD_DIM + n_blocks, :] = bias.astype(_BF16)
        qaug_scr[hd, HEAD_DIM + n_blocks:AUG_DIM, :] = jnp.zeros(
            (AUG_DIM - HEAD_DIM - n_blocks, tq), _BF16)

        sd_scr[hd] = _dot(k_ref[pl.ds(own_start, MOBA_BLOCK), sl], qt)
        m_scr[hd] = jnp.full((1, tq), _NEG_INF, _F32)
        l_scr[hd] = jnp.zeros((1, tq), _F32)
        acc_scr[hd] = jnp.zeros((HEAD_DIM, tq), _F32)

    lane = lax.broadcasted_iota(jnp.int32, (MOBA_BLOCK, HEAD_DIM), 1)

    def masked_scores(j, hd):
        sl = slice(hd * HEAD_DIM, (hd + 1) * HEAD_DIM)
        start = pl.multiple_of(j * MOBA_BLOCK, MOBA_BLOCK)
        onehot = jnp.where(lane == j, 1.0, 0.0).astype(_BF16)
        kaug = jnp.concatenate([k_ref[pl.ds(start, MOBA_BLOCK), sl], onehot], axis=1)
        return _dot(kaug, qaug_scr[hd])

    def softmax_pv(j, hd, s_ref, causal=False):
        sl = slice(hd * HEAD_DIM, (hd + 1) * HEAD_DIM)
        ps, alphas = [], []
        for half in range(tq // HEAD_DIM):
            cs = slice(half * HEAD_DIM, (half + 1) * HEAD_DIM)
            s = s_ref[hd, :, cs]
            if causal:
                s = jnp.where(key_i <= qry_i + half * HEAD_DIM, s, _NEG_INF)
            m_old = m_scr[hd, :, cs]
            m_new = jnp.maximum(m_old, jnp.max(s, axis=0, keepdims=True))
            alpha = jnp.exp2(m_old - m_new)
            p = jnp.exp2(s - m_new)
            m_scr[hd, :, cs] = m_new
            l_scr[hd, :, cs] = alpha * l_scr[hd, :, cs] + jnp.sum(p, axis=0, keepdims=True)
            ps.append(p.astype(_BF16))
            alphas.append(alpha)
        pv = _dot(vt_ref[j, sl, :], jnp.concatenate(ps, axis=1))
        for half, alpha in enumerate(alphas):
            cs = slice(half * HEAD_DIM, (half + 1) * HEAD_DIM)
            acc_scr[hd, :, cs] = alpha * acc_scr[hd, :, cs] + pv[:, cs]

    for hd in range(MOBA_HEADS):
        sa_scr[hd] = masked_scores(0, hd)
        sb_scr[hd] = masked_scores(1, hd)

    def pair_body(jj, carry):
        j0 = 2 * jj
        for hd in range(MOBA_HEADS):
            s_next = masked_scores(j0 + 2, hd)
            softmax_pv(j0, hd, sa_scr)
            sa_scr[hd] = s_next
        j3 = jnp.minimum(j0 + 3, n_blocks - 1)
        for hd in range(MOBA_HEADS):
            s_next = masked_scores(j3, hd)
            softmax_pv(j0 + 1, hd, sb_scr)
            sb_scr[hd] = s_next
        return carry

    lax.fori_loop(0, own // 2, pair_body, 0)

    @pl.when(own % 2 == 1)
    def _():
        for hd in range(MOBA_HEADS):
            softmax_pv(own - 1, hd, sa_scr)

    for hd in range(MOBA_HEADS):
        softmax_pv(own, hd, sd_scr, causal=True)

    for hd in range(MOBA_HEADS):
        sl = slice(hd * HEAD_DIM, (hd + 1) * HEAD_DIM)
        o = (acc_scr[hd] / l_scr[hd]).T
        outa_scr[:, sl] = (o * sga_ref[:, sl].astype(_F32)).astype(_BF16)

    ya = _dot(outa_scr[...], wa_ref[...])
    y = sma_ref[...].astype(_F32) * ya + ybc_ref[...].astype(_F32)
    z = x_ref[...] + _dot(y.astype(_BF16), wo_ref[...])
    o_ref[...] = _rms_norm(z, fg_ref[...])


def _rope_tables(seq):
    inv_freq = np.float32(ROPE_THETA) ** (-np.arange(0, ROT_DIM, 2, dtype=np.float32) / ROT_DIM)
    ang = np.arange(seq).astype(np.float32)[:, None] * inv_freq[None, :].astype(np.float32)
    cos, sin = np.cos(ang), np.sin(ang)
    rest = HEAD_DIM - ROT_DIM
    cos_t = np.concatenate([cos, cos, np.ones((seq, rest), np.float32)], axis=1)
    sin_t = np.concatenate([-sin, sin, np.zeros((seq, rest), np.float32)], axis=1)
    return jnp.asarray(cos_t, _F32), jnp.asarray(sin_t, _F32)


def _const_spec(shape):
    nd = len(shape)
    return pl.BlockSpec(shape, lambda *_: (0,) * nd, pipeline_mode=pl.Buffered(1))


def kernel(x, mem, norm_g, mem_norm_g, final_norm_g, w_in, w_mem_kv, gmlp_ln_g,
           w_spatial, b_spatial, w_branch_a, w_branch_b, w_branch_c, w_out):
    batch, seq, d_model = x.shape
    mem_len = mem.shape[1]
    assert w_in.shape[0] == 1, "single layer"
    assert seq % PROJ_ROWS == 0 and PROJ_ROWS % MOBA_BLOCK == 0 and PROJ_ROWS % GMLP_CHUNK == 0
    assert w_spatial.shape[1:] == (GMLP_GROUPS, GMLP_CHUNK, GMLP_CHUNK)
    n_tok = batch * seq
    n_blocks = seq // MOBA_BLOCK
    assert HEAD_DIM + n_blocks <= AUG_DIM and n_blocks % 16 == 0
    in_width = w_in.shape[-1]

    x2 = x.reshape(n_tok, d_model)
    w_in_b = w_in[0].astype(_BF16)
    cos_t, sin_t = _rope_tables(seq)
    bsp_t = jnp.transpose(b_spatial[0])

    params = functools.partial(pltpu.CompilerParams, vmem_limit_bytes=V7X_VMEM_LIMIT_BYTES)

    mk, mv = pl.pallas_call(
        _mem_kv_kernel,
        grid=(batch,),
        in_specs=[
            pl.BlockSpec((None, mem_len, d_model), lambda b: (b, 0, 0)),
            pl.BlockSpec((1, d_model), lambda b: (0, 0)),
            pl.BlockSpec((d_model, 2 * C_WIDTH), lambda b: (0, 0)),
        ],
        out_specs=[
            pl.BlockSpec((None, mem_len, C_WIDTH), lambda b: (b, 0, 0)),
            pl.BlockSpec((None, mem_len, C_WIDTH), lambda b: (b, 0, 0)),
        ],
        out_shape=[jax.ShapeDtypeStruct((batch, mem_len, C_WIDTH), _BF16)] * 2,
        compiler_params=params(dimension_semantics=("arbitrary",)),
        name="mem_kv",
    )(mem, mem_norm_g[0].reshape(1, d_model), w_mem_kv[0].astype(_BF16))

    tiles_per_seq = seq // PROJ_ROWS
    blocks_per_tile = PROJ_ROWS // MOBA_BLOCK
    row_spec = lambda w: pl.BlockSpec((PROJ_ROWS, w), lambda i: (i, 0))
    blk_t_spec = pl.BlockSpec((blocks_per_tile, A_WIDTH, MOBA_BLOCK), lambda i: (i, 0, 0))
    pos_spec = pl.BlockSpec((PROJ_ROWS, HEAD_DIM), lambda i: (i % tiles_per_seq, 0))
    mem_spec = pl.BlockSpec((None, mem_len, C_WIDTH), lambda i: (i // tiles_per_seq, 0, 0))
    blk_t_shape = jax.ShapeDtypeStruct((n_tok // MOBA_BLOCK, A_WIDTH, MOBA_BLOCK), _BF16)
    qt, ka, vt, sga, kmean, ybc, sma = pl.pallas_call(
        _proj_kernel,
        grid=(n_tok // PROJ_ROWS,),
        in_specs=[
            row_spec(d_model),
            _const_spec((1, d_model)),
            _const_spec((d_model, in_width)),
            pos_spec, pos_spec,
            _const_spec((1, B_WIDTH)),
            _const_spec((GMLP_GROUPS, GMLP_CHUNK, GMLP_CHUNK)),
            _const_spec((GMLP_CHUNK, GMLP_GROUPS)),
            mem_spec, mem_spec,
            _const_spec((B_WIDTH, d_model)),
            _const_spec((C_WIDTH, d_model)),
        ],
        out_specs=[
            blk_t_spec, row_spec(A_WIDTH), blk_t_spec, row_spec(A_WIDTH),
            pl.BlockSpec((None, blocks_per_tile, A_WIDTH), lambda i: (i, 0, 0)),
            row_spec(d_model), row_spec(d_model),
        ],
        out_shape=[
            blk_t_shape,
            jax.ShapeDtypeStruct((n_tok, A_WIDTH), _BF16),
            blk_t_shape,
            jax.ShapeDtypeStruct((n_tok, A_WIDTH), _BF16),
            jax.ShapeDtypeStruct((n_tok // PROJ_ROWS, blocks_per_tile, A_WIDTH), _F32),
            jax.ShapeDtypeStruct((n_tok, d_model), _BF16),
            jax.ShapeDtypeStruct((n_tok, d_model), _BF16),
        ],
        scratch_shapes=[
            pltpu.VMEM((PROJ_ROWS, B_WIDTH), _BF16),
            pltpu.VMEM((PROJ_ROWS, C_WIDTH), _BF16),
        ],
        compiler_params=params(dimension_semantics=("arbitrary",)),
        name="proj_branches",
    )(x2, norm_g[0].reshape(1, d_model), w_in_b, cos_t, sin_t,
      gmlp_ln_g[0].reshape(1, B_WIDTH), w_spatial[0], bsp_t, mk, mv,
      w_branch_b[0].astype(_BF16), w_branch_c[0].astype(_BF16))

    kmean = kmean.reshape(batch, n_blocks, A_WIDTH)
    k3 = ka.reshape(batch, seq, A_WIDTH)
    vt4 = vt.reshape(batch, n_blocks, A_WIDTH, MOBA_BLOCK)
    tile_spec = lambda w: pl.BlockSpec((MOBA_BLOCK, w), lambda b, i: (b * n_blocks + i, 0))
    out = pl.pallas_call(
        _moba_out_kernel,
        grid=(batch, n_blocks),
        in_specs=[
            pl.BlockSpec((None, A_WIDTH, MOBA_BLOCK), lambda b, i: (b * n_blocks + i, 0, 0)),
            pl.BlockSpec((None, seq, A_WIDTH), lambda b, i: (b, 0, 0),
                         pipeline_mode=pl.Buffered(1)),
            pl.BlockSpec((None, n_blocks, A_WIDTH, MOBA_BLOCK), lambda b, i: (b, 0, 0, 0),
                         pipeline_mode=pl.Buffered(1)),
            pl.BlockSpec((None, n_blocks, A_WIDTH), lambda b, i: (b, 0, 0)),
            tile_spec(A_WIDTH), tile_spec(d_model), tile_spec(d_model), tile_spec(d_model),
            _const_spec((A_WIDTH, d_model)),
            _const_spec((d_model, d_model)),
            _const_spec((1, d_model)),
        ],
        out_specs=tile_spec(d_model),
        out_shape=jax.ShapeDtypeStruct((n_tok, d_model), _F32),
        scratch_shapes=[
            pltpu.VMEM((MOBA_HEADS, AUG_DIM, MOBA_BLOCK), _BF16),
            pltpu.VMEM((MOBA_HEADS, MOBA_BLOCK, MOBA_BLOCK), _F32),
            pltpu.VMEM((MOBA_HEADS, MOBA_BLOCK, MOBA_BLOCK), _F32),
            pltpu.VMEM((MOBA_HEADS, MOBA_BLOCK, MOBA_BLOCK), _F32),
            pltpu.VMEM((MOBA_HEADS, 1, MOBA_BLOCK), _F32),
            pltpu.VMEM((MOBA_HEADS, 1, MOBA_BLOCK), _F32),
            pltpu.VMEM((MOBA_HEADS, HEAD_DIM, MOBA_BLOCK), _F32),
            pltpu.VMEM((MOBA_BLOCK, A_WIDTH), _BF16),
        ],
        compiler_params=params(dimension_semantics=("arbitrary", "arbitrary")),
        name="moba_merge_out",
    )(qt, k3, vt4, kmean, sga, ybc, sma, x2,
      w_branch_a[0].astype(_BF16), w_out[0].astype(_BF16), final_norm_g.reshape(1, d_model))
    return out.reshape(batch, seq, d_model)
```

```python
import functools
import math

import jax
import jax.numpy as jnp
import numpy as np
from jax import lax
from jax.experimental import pallas as pl
from jax.experimental.pallas import tpu as pltpu

HEAD_DIM = 128
MOBA_HEADS = 6
MOBA_BLOCK = 256
MOBA_TOPK = 3
GMLP_GROUPS = 6
GMLP_CHUNK = 128
XATTN_HEADS = 4
ROPE_THETA = 500000.0
ROT_DIM = HEAD_DIM // 4
EPS = 1e-6

A_WIDTH = MOBA_HEADS * HEAD_DIM
B_WIDTH = GMLP_GROUPS * HEAD_DIM
C_WIDTH = XATTN_HEADS * HEAD_DIM

V7X_VMEM_LIMIT_BYTES = 58 * 1024 * 1024

PROJ_ROWS = 512

QK_LOG2_SCALE = HEAD_DIM ** -0.5 * math.log2(math.e)
MASK_BIAS = -1e30
AUG_DIM = 2 * HEAD_DIM

_BF16 = jnp.bfloat16
_F32 = jnp.float32
_NEG_INF = float("-inf")


def _rms_norm(x, g):
    return x * lax.rsqrt(jnp.mean(x * x, axis=-1, keepdims=True) + EPS) * g


def _silu(x):
    return x * jax.nn.sigmoid(x)


def _dot(a, b):
    return jnp.dot(a, b, preferred_element_type=_F32)


def _dot_nt(a, b):
    return lax.dot_general(a, b, (((1,), (1,)), ((), ())), preferred_element_type=_F32)


def _mem_kv_kernel(mem_ref, g_ref, w_ref, mk_ref, mv_ref):
    mn = _rms_norm(mem_ref[...], g_ref[...]).astype(_BF16)
    kv = _dot(mn, w_ref[...])
    mk_ref[...] = kv[:, :C_WIDTH].astype(_BF16)
    mv_ref[...] = kv[:, C_WIDTH:].astype(_BF16)


def _proj_kernel(x_ref, g_ref, w_ref, cos_ref, sin_ref, lng_ref, wsp_ref, bsp_ref,
                 mk_ref, mv_ref, wb_ref, wc_ref,
                 qt_ref, k_ref, vt_ref, sga_ref, kmean_ref, ybc_ref, sma_ref,
                 outb_scr, outc_scr):
    rows = x_ref.shape[0]
    d_model = x_ref.shape[1]
    h = _rms_norm(x_ref[...], g_ref[...]).astype(_BF16)

    off = [0]

    def proj(width):
        a = off[0]
        off[0] = a + width
        return _dot(h, w_ref[:, a:a + width])

    cos = cos_ref[...]
    sin = sin_ref[...]
    lane = lax.broadcasted_iota(jnp.int32, (rows, HEAD_DIM), 1)
    first_half = lane < (ROT_DIM // 2)

    def rope(t):
        swapped = jnp.where(first_half,
                            pltpu.roll(t, HEAD_DIM - ROT_DIM // 2, 1),
                            pltpu.roll(t, ROT_DIM // 2, 1))
        return t * cos + swapped * sin

    qa = proj(A_WIDTH)
    for hd in range(MOBA_HEADS):
        sl = slice(hd * HEAD_DIM, (hd + 1) * HEAD_DIM)
        qh = rope(qa[:, sl]) * QK_LOG2_SCALE
        for r in range(rows // MOBA_BLOCK):
            rs = slice(r * MOBA_BLOCK, (r + 1) * MOBA_BLOCK)
            qt_ref[r, sl, :] = qh[rs].T.astype(_BF16)
    ka = proj(A_WIDTH)
    for hd in range(MOBA_HEADS):
        sl = slice(hd * HEAD_DIM, (hd + 1) * HEAD_DIM)
        kh = rope(ka[:, sl])
        k_ref[:, sl] = kh.astype(_BF16)
        kmean_ref[:, sl] = jnp.mean(
            kh.reshape(rows // MOBA_BLOCK, MOBA_BLOCK, HEAD_DIM), axis=1)
    va = proj(A_WIDTH)
    for r in range(rows // MOBA_BLOCK):
        rs = slice(r * MOBA_BLOCK, (r + 1) * MOBA_BLOCK)
        vt_ref[r] = va[rs].T.astype(_BF16)
    sga_ref[...] = _silu(proj(A_WIDTH)).astype(_BF16)

    ub = proj(B_WIDTH)
    vb = proj(B_WIDTH)
    gb = proj(B_WIDTH)
    mu = jnp.mean(vb, axis=-1, keepdims=True)
    var = jnp.mean(jnp.square(vb - mu), axis=-1, keepdims=True)
    vn = ((vb - mu) * lax.rsqrt(var + EPS) * lng_ref[...]).astype(_BF16)
    gate_b = ub * _silu(gb)
    r_i = lax.broadcasted_iota(jnp.int32, (GMLP_CHUNK, GMLP_CHUNK), 0)
    c_i = lax.broadcasted_iota(jnp.int32, (GMLP_CHUNK, GMLP_CHUNK), 1)
    causal = r_i >= c_i
    bsp = bsp_ref[...]
    chunks = [slice(c * GMLP_CHUNK, (c + 1) * GMLP_CHUNK) for c in range(rows // GMLP_CHUNK)]
    for g in range(GMLP_GROUPS):
        sl = slice(g * HEAD_DIM, (g + 1) * HEAD_DIM)
        wg = jnp.where(causal, wsp_ref[g], 0.0).astype(_BF16)
        mixed = _dot(wg, jnp.concatenate([vn[rs, sl] for rs in chunks], axis=1))
        bias = bsp[:, g:g + 1]
        for c, rs in enumerate(chunks):
            mixed_c = mixed[:, c * HEAD_DIM:(c + 1) * HEAD_DIM] + bias
            outb_scr[rs, sl] = (gate_b[rs, sl] * mixed_c).astype(_BF16)

    qc = proj(C_WIDTH).astype(_BF16)
    gc = proj(C_WIDTH)
    scale = HEAD_DIM ** -0.5
    for hd in range(XATTN_HEADS):
        sl = slice(hd * HEAD_DIM, (hd + 1) * HEAD_DIM)
        s = _dot_nt(qc[:, sl], mk_ref[:, sl]) * scale
        m = jnp.max(s, axis=-1, keepdims=True)
        p = jnp.exp(s - m)
        l = jnp.sum(p, axis=-1, keepdims=True)
        o = _dot(p.astype(_BF16), mv_ref[:, sl]) / l
        outc_scr[:, sl] = (o * _silu(gc[:, sl])).astype(_BF16)

    sma_ref[...] = jax.nn.sigmoid(proj(d_model)).astype(_BF16)
    yb = jax.nn.sigmoid(proj(d_model)) * _dot(outb_scr[...], wb_ref[...])
    yc = jax.nn.sigmoid(proj(d_model)) * _dot(outc_scr[...], wc_ref[...])
    ybc_ref[...] = (yb + yc).astype(_BF16)


def _moba_out_kernel(qt_ref, k_hbm, vt_hbm, kmean_ref, sga_ref, ybc_ref, sma_ref, x_ref,
                     wa_ref, wo_ref, fg_ref, o_ref,
                     k_ref, vt_ref, kv_sem,
                     qaug_scr, sa_scr, sb_scr, sd_scr, m_scr, l_scr, acc_scr, outa_scr):
    batch_i = pl.program_id(0)
    own = pl.program_id(1)
    n_batch = pl.num_programs(0)
    n_blocks = kmean_ref.shape[0]
    tq = qt_ref.shape[1]

    def kv_copies(b, j):
        rows = pl.ds(pl.multiple_of(j * MOBA_BLOCK, MOBA_BLOCK), MOBA_BLOCK)
        return (pltpu.make_async_copy(k_hbm.at[b, rows, :], k_ref.at[rows, :],
                                      kv_sem.at[0, j % 2]),
                pltpu.make_async_copy(vt_hbm.at[b, j], vt_ref.at[j], kv_sem.at[1, j % 2]))

    def kv_start(b, j):
        for copy in kv_copies(b, j):
            copy.start()

    def kv_wait(b, j):
        for copy in kv_copies(b, j):
            copy.wait()

    @pl.when(jnp.logical_and(batch_i == 0, own == 0))
    def _():
        kv_start(0, 0)
        kv_start(0, 1)

    @pl.when(own == 0)
    def _():
        kv_wait(batch_i, 0)
        kv_wait(batch_i, 1)

    @pl.when(jnp.logical_and(own >= 1, own + 1 < n_blocks))
    def _():
        kv_wait(batch_i, own + 1)

    @pl.when(own + 2 < n_blocks)
    def _():
        kv_start(batch_i, own + 2)

    blk = lax.broadcasted_iota(jnp.int32, (n_blocks, tq), 0)
    valid = blk < own
    key_i = lax.broadcasted_iota(jnp.int32, (MOBA_BLOCK, HEAD_DIM), 0)
    qry_i = lax.broadcasted_iota(jnp.int32, (MOBA_BLOCK, HEAD_DIM), 1)
    own_start = pl.multiple_of(own * MOBA_BLOCK, MOBA_BLOCK)

    for hd in range(MOBA_HEADS):
        sl = slice(hd * HEAD_DIM, (hd + 1) * HEAD_DIM)
        qt = qt_ref[sl, :]

        gs = _dot(kmean_ref[:, sl].astype(_BF16), qt)
        gs = jnp.where(valid, gs, _NEG_INF)
        sel = jnp.zeros((n_blocks, tq), dtype=jnp.bool_)
        for _ in range(MOBA_TOPK):
            top = jnp.max(gs, axis=0, keepdims=True)
            idx = jnp.min(jnp.where(gs == top, blk, n_blocks), axis=0, keepdims=True)
            pick = blk == idx
            sel = jnp.logical_or(sel, pick)
            gs = jnp.where(pick, _NEG_INF, gs)
        bias = jnp.where(jnp.logical_and(sel, valid), 0.0, MASK_BIAS)
        qaug_scr[hd, 0:HEAD_DIM, :] = qt
        qaug_scr[hd, HEAD_DIM:HEAD_DIM + n_blocks, :] = bias.astype(_BF16)
        qaug_scr[hd, HEAD_DIM + n_blocks:AUG_DIM, :] = jnp.zeros(
            (AUG_DIM - HEAD_DIM - n_blocks, tq), _BF16)

        sd_scr[hd] = _dot(k_ref[pl.ds(own_start, MOBA_BLOCK), sl], qt)
        m_scr[hd] = jnp.full((1, tq), _NEG_INF, _F32)
        l_scr[hd] = jnp.zeros((1, tq), _F32)
        acc_scr[hd] = jnp.zeros((HEAD_DIM, tq), _F32)

    lane = lax.broadcasted_iota(jnp.int32, (MOBA_BLOCK, HEAD_DIM), 1)

    def masked_scores(j, hd):
        sl = slice(hd * HEAD_DIM, (hd + 1) * HEAD_DIM)
        start = pl.multiple_of(j * MOBA_BLOCK, MOBA_BLOCK)
        onehot = jnp.where(lane == j, 1.0, 0.0).astype(_BF16)
        kaug = jnp.concatenate([k_ref[pl.ds(start, MOBA_BLOCK), sl], onehot], axis=1)
        return _dot(kaug, qaug_scr[hd])

    def softmax_pv(j, hd, s_ref, causal=False):
        sl = slice(hd * HEAD_DIM, (hd + 1) * HEAD_DIM)
        ps, alphas = [], []
        for half in range(tq // HEAD_DIM):
            cs = slice(half * HEAD_DIM, (half + 1) * HEAD_DIM)
            s = s_ref[hd, :, cs]
            if causal:
                s = jnp.where(key_i <= qry_i + half * HEAD_DIM, s, _NEG_INF)
            m_old = m_scr[hd, :, cs]
            m_new = jnp.maximum(m_old, jnp.max(s, axis=0, keepdims=True))
            alpha = jnp.exp2(m_old - m_new)
            p = jnp.exp2(s - m_new)
            m_scr[hd, :, cs] = m_new
            l_scr[hd, :, cs] = alpha * l_scr[hd, :, cs] + jnp.sum(p, axis=0, keepdims=True)
            ps.append(p.astype(_BF16))
            alphas.append(alpha)
        pv = _dot(vt_ref[j, sl, :], jnp.concatenate(ps, axis=1))
        for half, alpha in enumerate(alphas):
            cs = slice(half * HEAD_DIM, (half + 1) * HEAD_DIM)
            acc_scr[hd, :, cs] = alpha * acc_scr[hd, :, cs] + pv[:, cs]

    for hd in range(MOBA_HEADS):
        sa_scr[hd] = masked_scores(0, hd)
        sb_scr[hd] = masked_scores(1, hd)

    def pair_body(jj, carry):
        j0 = 2 * jj
        for hd in range(MOBA_HEADS):
            s_next = masked_scores(j0 + 2, hd)
            softmax_pv(j0, hd, sa_scr)
            sa_scr[hd] = s_next
        j3 = jnp.minimum(j0 + 3, n_blocks - 1)
        for hd in range(MOBA_HEADS):
            s_next = masked_scores(j3, hd)
            softmax_pv(j0 + 1, hd, sb_scr)
            sb_scr[hd] = s_next
        return carry

    lax.fori_loop(0, own // 2, pair_body, 0)

    @pl.when(own % 2 == 1)
    def _():
        for hd in range(MOBA_HEADS):
            softmax_pv(own - 1, hd, sa_scr)

    for hd in range(MOBA_HEADS):
        softmax_pv(own, hd, sd_scr, causal=True)

    for hd in range(MOBA_HEADS):
        sl = slice(hd * HEAD_DIM, (hd + 1) * HEAD_DIM)
        o = (acc_scr[hd] / l_scr[hd]).T
        outa_scr[:, sl] = (o * sga_ref[:, sl].astype(_F32)).astype(_BF16)

    ya = _dot(outa_scr[...], wa_ref[...])
    y = sma_ref[...].astype(_F32) * ya + ybc_ref[...].astype(_F32)
    z = x_ref[...] + _dot(y.astype(_BF16), wo_ref[...])
    o_ref[...] = _rms_norm(z, fg_ref[...])

    @pl.when(jnp.logical_and(own == n_blocks - 1, batch_i + 1 < n_batch))
    def _():
        kv_start(batch_i + 1, 0)
        kv_start(batch_i + 1, 1)


def _rope_tables(seq):
    inv_freq = np.float32(ROPE_THETA) ** (-np.arange(0, ROT_DIM, 2, dtype=np.float32) / ROT_DIM)
    ang = np.arange(seq).astype(np.float32)[:, None] * inv_freq[None, :].astype(np.float32)
    cos, sin = np.cos(ang), np.sin(ang)
    rest = HEAD_DIM - ROT_DIM
    cos_t = np.concatenate([cos, cos, np.ones((seq, rest), np.float32)], axis=1)
    sin_t = np.concatenate([-sin, sin, np.zeros((seq, rest), np.float32)], axis=1)
    return jnp.asarray(cos_t, _F32), jnp.asarray(sin_t, _F32)


def _const_spec(shape):
    nd = len(shape)
    return pl.BlockSpec(shape, lambda *_: (0,) * nd, pipeline_mode=pl.Buffered(1))


def kernel(x, mem, norm_g, mem_norm_g, final_norm_g, w_in, w_mem_kv, gmlp_ln_g,
           w_spatial, b_spatial, w_branch_a, w_branch_b, w_branch_c, w_out):
    batch, seq, d_model = x.shape
    mem_len = mem.shape[1]
    assert w_in.shape[0] == 1, "single layer"
    assert seq % PROJ_ROWS == 0 and PROJ_ROWS % MOBA_BLOCK == 0 and PROJ_ROWS % GMLP_CHUNK == 0
    assert w_spatial.shape[1:] == (GMLP_GROUPS, GMLP_CHUNK, GMLP_CHUNK)
    n_tok = batch * seq
    n_blocks = seq // MOBA_BLOCK
    assert HEAD_DIM + n_blocks <= AUG_DIM and n_blocks % 16 == 0
    in_width = w_in.shape[-1]

    x2 = x.reshape(n_tok, d_model)
    w_in_b = w_in[0].astype(_BF16)
    cos_t, sin_t = _rope_tables(seq)
    bsp_t = jnp.transpose(b_spatial[0])

    params = functools.partial(pltpu.CompilerParams, vmem_limit_bytes=V7X_VMEM_LIMIT_BYTES)

    mk, mv = pl.pallas_call(
        _mem_kv_kernel,
        grid=(batch,),
        in_specs=[
            pl.BlockSpec((None, mem_len, d_model), lambda b: (b, 0, 0)),
            pl.BlockSpec((1, d_model), lambda b: (0, 0)),
            pl.BlockSpec((d_model, 2 * C_WIDTH), lambda b: (0, 0)),
        ],
        out_specs=[
            pl.BlockSpec((None, mem_len, C_WIDTH), lambda b: (b, 0, 0)),
            pl.BlockSpec((None, mem_len, C_WIDTH), lambda b: (b, 0, 0)),
        ],
        out_shape=[jax.ShapeDtypeStruct((batch, mem_len, C_WIDTH), _BF16)] * 2,
        compiler_params=params(dimension_semantics=("arbitrary",)),
        name="mem_kv",
    )(mem, mem_norm_g[0].reshape(1, d_model), w_mem_kv[0].astype(_BF16))

    tiles_per_seq = seq // PROJ_ROWS
    blocks_per_tile = PROJ_ROWS // MOBA_BLOCK
    row_spec = lambda w: pl.BlockSpec((PROJ_ROWS, w), lambda i: (i, 0))
    blk_t_spec = pl.BlockSpec((blocks_per_tile, A_WIDTH, MOBA_BLOCK), lambda i: (i, 0, 0))
    pos_spec = pl.BlockSpec((PROJ_ROWS, HEAD_DIM), lambda i: (i % tiles_per_seq, 0))
    mem_spec = pl.BlockSpec((None, mem_len, C_WIDTH), lambda i: (i // tiles_per_seq, 0, 0))
    blk_t_shape = jax.ShapeDtypeStruct((n_tok // MOBA_BLOCK, A_WIDTH, MOBA_BLOCK), _BF16)
    qt, ka, vt, sga, kmean, ybc, sma = pl.pallas_call(
        _proj_kernel,
        grid=(n_tok // PROJ_ROWS,),
        in_specs=[
            row_spec(d_model),
            _const_spec((1, d_model)),
            _const_spec((d_model, in_width)),
            pos_spec, pos_spec,
            _const_spec((1, B_WIDTH)),
            _const_spec((GMLP_GROUPS, GMLP_CHUNK, GMLP_CHUNK)),
            _const_spec((GMLP_CHUNK, GMLP_GROUPS)),
            mem_spec, mem_spec,
            _const_spec((B_WIDTH, d_model)),
            _const_spec((C_WIDTH, d_model)),
        ],
        out_specs=[
            blk_t_spec, row_spec(A_WIDTH), blk_t_spec, row_spec(A_WIDTH),
            pl.BlockSpec((None, blocks_per_tile, A_WIDTH), lambda i: (i, 0, 0)),
            row_spec(d_model), row_spec(d_model),
        ],
        out_shape=[
            blk_t_shape,
            jax.ShapeDtypeStruct((n_tok, A_WIDTH), _BF16),
            blk_t_shape,
            jax.ShapeDtypeStruct((n_tok, A_WIDTH), _BF16),
            jax.ShapeDtypeStruct((n_tok // PROJ_ROWS, blocks_per_tile, A_WIDTH), _F32),
            jax.ShapeDtypeStruct((n_tok, d_model), _BF16),
            jax.ShapeDtypeStruct((n_tok, d_model), _BF16),
        ],
        scratch_shapes=[
            pltpu.VMEM((PROJ_ROWS, B_WIDTH), _BF16),
            pltpu.VMEM((PROJ_ROWS, C_WIDTH), _BF16),
        ],
        compiler_params=params(dimension_semantics=("arbitrary",)),
        name="proj_branches",
    )(x2, norm_g[0].reshape(1, d_model), w_in_b, cos_t, sin_t,
      gmlp_ln_g[0].reshape(1, B_WIDTH), w_spatial[0], bsp_t, mk, mv,
      w_branch_b[0].astype(_BF16), w_branch_c[0].astype(_BF16))

    kmean = kmean.reshape(batch, n_blocks, A_WIDTH)
    k3 = ka.reshape(batch, seq, A_WIDTH)
    vt4 = vt.reshape(batch, n_blocks, A_WIDTH, MOBA_BLOCK)
    tile_spec = lambda w: pl.BlockSpec((MOBA_BLOCK, w), lambda b, i: (b * n_blocks + i, 0))
    out = pl.pallas_call(
        _moba_out_kernel,
        grid=(batch, n_blocks),
        in_specs=[
            pl.BlockSpec((None, A_WIDTH, MOBA_BLOCK), lambda b, i: (b * n_blocks + i, 0, 0)),
            pl.BlockSpec(memory_space=pl.ANY),
            pl.BlockSpec(memory_space=pl.ANY),
            pl.BlockSpec((None, n_blocks, A_WIDTH), lambda b, i: (b, 0, 0)),
            tile_spec(A_WIDTH), tile_spec(d_model), tile_spec(d_model), tile_spec(d_model),
            _const_spec((A_WIDTH, d_model)),
            _const_spec((d_model, d_model)),
            _const_spec((1, d_model)),
        ],
        out_specs=tile_spec(d_model),
        out_shape=jax.ShapeDtypeStruct((n_tok, d_model), _F32),
        scratch_shapes=[
            pltpu.VMEM((seq, A_WIDTH), _BF16),
            pltpu.VMEM((n_blocks, A_WIDTH, MOBA_BLOCK), _BF16),
            pltpu.SemaphoreType.DMA((2, 2)),
            pltpu.VMEM((MOBA_HEADS, AUG_DIM, MOBA_BLOCK), _BF16),
            pltpu.VMEM((MOBA_HEADS, MOBA_BLOCK, MOBA_BLOCK), _F32),
            pltpu.VMEM((MOBA_HEADS, MOBA_BLOCK, MOBA_BLOCK), _F32),
            pltpu.VMEM((MOBA_HEADS, MOBA_BLOCK, MOBA_BLOCK), _F32),
            pltpu.VMEM((MOBA_HEADS, 1, MOBA_BLOCK), _F32),
            pltpu.VMEM((MOBA_HEADS, 1, MOBA_BLOCK), _F32),
            pltpu.VMEM((MOBA_HEADS, HEAD_DIM, MOBA_BLOCK), _F32),
            pltpu.VMEM((MOBA_BLOCK, A_WIDTH), _BF16),
        ],
        compiler_params=params(dimension_semantics=("arbitrary", "arbitrary")),
        name="moba_merge_out",
    )(qt, k3, vt4, kmean, sga, ybc, sma, x2,
      w_branch_a[0].astype(_BF16), w_out[0].astype(_BF16), final_norm_g.reshape(1, d_model))
    return out.reshape(batch, seq, d_model)
```

```python
import functools
import math

import jax
import jax.numpy as jnp
import numpy as np
from jax import lax
from jax.experimental import pallas as pl
from jax.experimental.pallas import tpu as pltpu

HEAD_DIM = 128
MOBA_HEADS = 6
MOBA_BLOCK = 256
MOBA_TOPK = 3
GMLP_GROUPS = 6
GMLP_CHUNK = 128
XATTN_HEADS = 4
ROPE_THETA = 500000.0
ROT_DIM = HEAD_DIM // 4
EPS = 1e-6

A_WIDTH = MOBA_HEADS * HEAD_DIM
B_WIDTH = GMLP_GROUPS * HEAD_DIM
C_WIDTH = XATTN_HEADS * HEAD_DIM

V7X_VMEM_LIMIT_BYTES = 58 * 1024 * 1024

PROJ_ROWS = 512
W_IN_CHUNK = 512

QK_LOG2_SCALE = HEAD_DIM ** -0.5 * math.log2(math.e)
MASK_BIAS = -1e30
AUG_DIM = 2 * HEAD_DIM

_BF16 = jnp.bfloat16
_F32 = jnp.float32
_NEG_INF = float("-inf")


def _rms_norm(x, g):
    return x * lax.rsqrt(jnp.mean(x * x, axis=-1, keepdims=True) + EPS) * g


def _silu(x):
    return x * jax.nn.sigmoid(x)


def _dot(a, b):
    return jnp.dot(a, b, preferred_element_type=_F32)


def _dot_nt(a, b):
    return lax.dot_general(a, b, (((1,), (1,)), ((), ())), preferred_element_type=_F32)


def _mem_kv_kernel(mem_ref, g_ref, w_ref, mk_ref, mv_ref):
    mn = _rms_norm(mem_ref[...], g_ref[...]).astype(_BF16)
    kv = _dot(mn, w_ref[...])
    mk_ref[...] = kv[:, :C_WIDTH].astype(_BF16)
    mv_ref[...] = kv[:, C_WIDTH:].astype(_BF16)


def _proj_kernel(x_ref, g_ref, w_hbm, cos_ref, sin_ref, lng_ref, wsp_ref, bsp_ref,
                 mk_ref, mv_ref, wb_ref, wc_ref,
                 qt_ref, k_ref, vt_ref, sga_ref, kmean_ref, ybc_ref, sma_ref,
                 w_ref, w_stage, w_sem, outb_scr, outc_scr):
    rows = x_ref.shape[0]
    d_model = x_ref.shape[1]

    @pl.when(pl.program_id(0) == 0)
    def _():
        in_width = w_ref.shape[1]
        chunk = w_stage.shape[2]
        starts = list(range(0, in_width, chunk))

        def w_copy(c):
            width = min(chunk, in_width - starts[c])
            return pltpu.make_async_copy(
                w_hbm.at[:, starts[c]:starts[c] + width],
                w_stage.at[c % 2, :, 0:width], w_sem.at[c % 2])

        w_copy(0).start()
        for c in range(len(starts)):
            if c + 1 < len(starts):
                w_copy(c + 1).start()
            w_copy(c).wait()
            width = min(chunk, in_width - starts[c])
            w_ref[:, starts[c]:starts[c] + width] = w_stage[c % 2, :, 0:width].astype(_BF16)

    h = _rms_norm(x_ref[...], g_ref[...]).astype(_BF16)

    off = [0]

    def proj(width):
        a = off[0]
        off[0] = a + width
        return _dot(h, w_ref[:, a:a + width])

    cos = cos_ref[...]
    sin = sin_ref[...]
    lane = lax.broadcasted_iota(jnp.int32, (rows, HEAD_DIM), 1)
    first_half = lane < (ROT_DIM // 2)

    def rope(t):
        swapped = jnp.where(first_half,
                            pltpu.roll(t, HEAD_DIM - ROT_DIM // 2, 1),
                            pltpu.roll(t, ROT_DIM // 2, 1))
        return t * cos + swapped * sin

    qa = proj(A_WIDTH)
    for hd in range(MOBA_HEADS):
        sl = slice(hd * HEAD_DIM, (hd + 1) * HEAD_DIM)
        qh = rope(qa[:, sl]) * QK_LOG2_SCALE
        for r in range(rows // MOBA_BLOCK):
            rs = slice(r * MOBA_BLOCK, (r + 1) * MOBA_BLOCK)
            qt_ref[r, sl, :] = qh[rs].T.astype(_BF16)
    ka = proj(A_WIDTH)
    for hd in range(MOBA_HEADS):
        sl = slice(hd * HEAD_DIM, (hd + 1) * HEAD_DIM)
        kh = rope(ka[:, sl])
        k_ref[:, sl] = kh.astype(_BF16)
        kmean_ref[:, sl] = jnp.mean(
            kh.reshape(rows // MOBA_BLOCK, MOBA_BLOCK, HEAD_DIM), axis=1)
    va = proj(A_WIDTH)
    for r in range(rows // MOBA_BLOCK):
        rs = slice(r * MOBA_BLOCK, (r + 1) * MOBA_BLOCK)
        vt_ref[r] = va[rs].T.astype(_BF16)
    sga_ref[...] = _silu(proj(A_WIDTH)).astype(_BF16)

    ub = proj(B_WIDTH)
    vb = proj(B_WIDTH)
    gb = proj(B_WIDTH)
    mu = jnp.mean(vb, axis=-1, keepdims=True)
    var = jnp.mean(jnp.square(vb - mu), axis=-1, keepdims=True)
    vn = ((vb - mu) * lax.rsqrt(var + EPS) * lng_ref[...]).astype(_BF16)
    gate_b = ub * _silu(gb)
    r_i = lax.broadcasted_iota(jnp.int32, (GMLP_CHUNK, GMLP_CHUNK), 0)
    c_i = lax.broadcasted_iota(jnp.int32, (GMLP_CHUNK, GMLP_CHUNK), 1)
    causal = r_i >= c_i
    bsp = bsp_ref[...]
    chunks = [slice(c * GMLP_CHUNK, (c + 1) * GMLP_CHUNK) for c in range(rows // GMLP_CHUNK)]
    for g in range(GMLP_GROUPS):
        sl = slice(g * HEAD_DIM, (g + 1) * HEAD_DIM)
        wg = jnp.where(causal, wsp_ref[g], 0.0).astype(_BF16)
        mixed = _dot(wg, jnp.concatenate([vn[rs, sl] for rs in chunks], axis=1))
        bias = bsp[:, g:g + 1]
        for c, rs in enumerate(chunks):
            mixed_c = mixed[:, c * HEAD_DIM:(c + 1) * HEAD_DIM] + bias
            outb_scr[rs, sl] = (gate_b[rs, sl] * mixed_c).astype(_BF16)

    qc = proj(C_WIDTH).astype(_BF16)
    gc = proj(C_WIDTH)
    scale = HEAD_DIM ** -0.5
    for hd in range(XATTN_HEADS):
        sl = slice(hd * HEAD_DIM, (hd + 1) * HEAD_DIM)
        s = _dot_nt(qc[:, sl], mk_ref[:, sl]) * scale
        m = jnp.max(s, axis=-1, keepdims=True)
        p = jnp.exp(s - m)
        l = jnp.sum(p, axis=-1, keepdims=True)
        o = _dot(p.astype(_BF16), mv_ref[:, sl]) / l
        outc_scr[:, sl] = (o * _silu(gc[:, sl])).astype(_BF16)

    sma_ref[...] = jax.nn.sigmoid(proj(d_model)).astype(_BF16)
    yb = jax.nn.sigmoid(proj(d_model)) * _dot(outb_scr[...], wb_ref[...])
    yc = jax.nn.sigmoid(proj(d_model)) * _dot(outc_scr[...], wc_ref[...])
    ybc_ref[...] = (yb + yc).astype(_BF16)


def _moba_out_kernel(qt_ref, k_hbm, vt_hbm, kmean_ref, sga_ref, ybc_ref, sma_ref, x_ref,
                     wa_ref, wo_ref, fg_ref, o_ref,
                     k_ref, vt_ref, kv_sem,
                     qaug_scr, sa_scr, sb_scr, sd_scr, m_scr, l_scr, acc_scr, outa_scr):
    batch_i = pl.program_id(0)
    own = pl.program_id(1)
    n_batch = pl.num_programs(0)
    n_blocks = kmean_ref.shape[0]
    tq = qt_ref.shape[1]

    def kv_copies(b, j):
        rows = pl.ds(pl.multiple_of(j * MOBA_BLOCK, MOBA_BLOCK), MOBA_BLOCK)
        return (pltpu.make_async_copy(k_hbm.at[b, rows, :], k_ref.at[rows, :],
                                      kv_sem.at[0, j % 2]),
                pltpu.make_async_copy(vt_hbm.at[b, j], vt_ref.at[j], kv_sem.at[1, j % 2]))

    def kv_start(b, j):
        for copy in kv_copies(b, j):
            copy.start()

    def kv_wait(b, j):
        for copy in kv_copies(b, j):
            copy.wait()

    @pl.when(jnp.logical_and(batch_i == 0, own == 0))
    def _():
        kv_start(0, 0)
        kv_start(0, 1)

    @pl.when(own == 0)
    def _():
        kv_wait(batch_i, 0)
        kv_wait(batch_i, 1)

    @pl.when(jnp.logical_and(own >= 1, own + 1 < n_blocks))
    def _():
        kv_wait(batch_i, own + 1)

    @pl.when(own + 2 < n_blocks)
    def _():
        kv_start(batch_i, own + 2)

    blk = lax.broadcasted_iota(jnp.int32, (n_blocks, tq), 0)
    valid = blk < own
    key_i = lax.broadcasted_iota(jnp.int32, (MOBA_BLOCK, HEAD_DIM), 0)
    qry_i = lax.broadcasted_iota(jnp.int32, (MOBA_BLOCK, HEAD_DIM), 1)
    own_start = pl.multiple_of(own * MOBA_BLOCK, MOBA_BLOCK)

    for hd in range(MOBA_HEADS):
        sl = slice(hd * HEAD_DIM, (hd + 1) * HEAD_DIM)
        qt = qt_ref[sl, :]

        gs = _dot(kmean_ref[:, sl].astype(_BF16), qt)
        gs = jnp.where(valid, gs, _NEG_INF)
        sel = jnp.zeros((n_blocks, tq), dtype=jnp.bool_)
        for _ in range(MOBA_TOPK):
            top = jnp.max(gs, axis=0, keepdims=True)
            idx = jnp.min(jnp.where(gs == top, blk, n_blocks), axis=0, keepdims=True)
            pick = blk == idx
            sel = jnp.logical_or(sel, pick)
            gs = jnp.where(pick, _NEG_INF, gs)
        bias = jnp.where(jnp.logical_and(sel, valid), 0.0, MASK_BIAS)
        qaug_scr[hd, 0:HEAD_DIM, :] = qt
        qaug_scr[hd, HEAD_DIM:HEAD_DIM + n_blocks, :] = bias.astype(_BF16)
        qaug_scr[hd, HEAD_DIM + n_blocks:AUG_DIM, :] = jnp.zeros(
            (AUG_DIM - HEAD_DIM - n_blocks, tq), _BF16)

        sd_scr[hd] = _dot(k_ref[pl.ds(own_start, MOBA_BLOCK), sl], qt)
        m_scr[hd] = jnp.full((1, tq), _NEG_INF, _F32)
        l_scr[hd] = jnp.zeros((1, tq), _F32)
        acc_scr[hd] = jnp.zeros((HEAD_DIM, tq), _F32)

    lane = lax.broadcasted_iota(jnp.int32, (MOBA_BLOCK, HEAD_DIM), 1)

    def masked_scores(j, hd):
        sl = slice(hd * HEAD_DIM, (hd + 1) * HEAD_DIM)
        start = pl.multiple_of(j * MOBA_BLOCK, MOBA_BLOCK)
        onehot = jnp.where(lane == j, 1.0, 0.0).astype(_BF16)
        kaug = jnp.concatenate([k_ref[pl.ds(start, MOBA_BLOCK), sl], onehot], axis=1)
        return _dot(kaug, qaug_scr[hd])

    def softmax_pv(j, hd, s_ref, causal=False):
        sl = slice(hd * HEAD_DIM, (hd + 1) * HEAD_DIM)
        ps, alphas = [], []
        for half in range(tq // HEAD_DIM):
            cs = slice(half * HEAD_DIM, (half + 1) * HEAD_DIM)
            s = s_ref[hd, :, cs]
            if causal:
                s = jnp.where(key_i <= qry_i + half * HEAD_DIM, s, _NEG_INF)
            m_old = m_scr[hd, :, cs]
            m_new = jnp.maximum(m_old, jnp.max(s, axis=0, keepdims=True))
            alpha = jnp.exp2(m_old - m_new)
            p = jnp.exp2(s - m_new)
            m_scr[hd, :, cs] = m_new
            l_scr[hd, :, cs] = alpha * l_scr[hd, :, cs] + jnp.sum(p, axis=0, keepdims=True)
            ps.append(p.astype(_BF16))
            alphas.append(alpha)
        pv = _dot(vt_ref[j, sl, :], jnp.concatenate(ps, axis=1))
        for half, alpha in enumerate(alphas):
            cs = slice(half * HEAD_DIM, (half + 1) * HEAD_DIM)
            acc_scr[hd, :, cs] = alpha * acc_scr[hd, :, cs] + pv[:, cs]

    for hd in range(MOBA_HEADS):
        sa_scr[hd] = masked_scores(0, hd)
        sb_scr[hd] = masked_scores(1, hd)

    def pair_body(jj, carry):
        j0 = 2 * jj
        for hd in range(MOBA_HEADS):
            s_next = masked_scores(j0 + 2, hd)
            softmax_pv(j0, hd, sa_scr)
            sa_scr[hd] = s_next
        j3 = jnp.minimum(j0 + 3, n_blocks - 1)
        for hd in range(MOBA_HEADS):
            s_next = masked_scores(j3, hd)
            softmax_pv(j0 + 1, hd, sb_scr)
            sb_scr[hd] = s_next
        return carry

    lax.fori_loop(0, own // 2, pair_body, 0)

    @pl.when(own % 2 == 1)
    def _():
        for hd in range(MOBA_HEADS):
            softmax_pv(own - 1, hd, sa_scr)

    for hd in range(MOBA_HEADS):
        softmax_pv(own, hd, sd_scr, causal=True)

    for hd in range(MOBA_HEADS):
        sl = slice(hd * HEAD_DIM, (hd + 1) * HEAD_DIM)
        o = (acc_scr[hd] / l_scr[hd]).T
        outa_scr[:, sl] = (o * sga_ref[:, sl].astype(_F32)).astype(_BF16)

    ya = _dot(outa_scr[...], wa_ref[...])
    y = sma_ref[...].astype(_F32) * ya + ybc_ref[...].astype(_F32)
    z = x_ref[...] + _dot(y.astype(_BF16), wo_ref[...])
    o_ref[...] = _rms_norm(z, fg_ref[...])

    @pl.when(jnp.logical_and(own == n_blocks - 1, batch_i + 1 < n_batch))
    def _():
        kv_start(batch_i + 1, 0)
        kv_start(batch_i + 1, 1)


def _rope_tables(seq):
    inv_freq = np.float32(ROPE_THETA) ** (-np.arange(0, ROT_DIM, 2, dtype=np.float32) / ROT_DIM)
    ang = np.arange(seq).astype(np.float32)[:, None] * inv_freq[None, :].astype(np.float32)
    cos, sin = np.cos(ang), np.sin(ang)
    rest = HEAD_DIM - ROT_DIM
    cos_t = np.concatenate([cos, cos, np.ones((seq, rest), np.float32)], axis=1)
    sin_t = np.concatenate([-sin, sin, np.zeros((seq, rest), np.float32)], axis=1)
    return jnp.asarray(cos_t, _F32), jnp.asarray(sin_t, _F32)


def _const_spec(shape):
    nd = len(shape)
    return pl.BlockSpec(shape, lambda *_: (0,) * nd, pipeline_mode=pl.Buffered(1))


def kernel(x, mem, norm_g, mem_norm_g, final_norm_g, w_in, w_mem_kv, gmlp_ln_g,
           w_spatial, b_spatial, w_branch_a, w_branch_b, w_branch_c, w_out):
    batch, seq, d_model = x.shape
    mem_len = mem.shape[1]
    assert w_in.shape[0] == 1, "single layer"
    assert seq % PROJ_ROWS == 0 and PROJ_ROWS % MOBA_BLOCK == 0 and PROJ_ROWS % GMLP_CHUNK == 0
    assert w_spatial.shape[1:] == (GMLP_GROUPS, GMLP_CHUNK, GMLP_CHUNK)
    n_tok = batch * seq
    n_blocks = seq // MOBA_BLOCK
    assert HEAD_DIM + n_blocks <= AUG_DIM and n_blocks % 16 == 0
    in_width = w_in.shape[-1]

    x2 = x.reshape(n_tok, d_model)
    cos_t, sin_t = _rope_tables(seq)
    bsp_t = jnp.transpose(b_spatial[0])

    params = functools.partial(pltpu.CompilerParams, vmem_limit_bytes=V7X_VMEM_LIMIT_BYTES)

    mk, mv = pl.pallas_call(
        _mem_kv_kernel,
        grid=(batch,),
        in_specs=[
            pl.BlockSpec((None, mem_len, d_model), lambda b: (b, 0, 0)),
            pl.BlockSpec((1, d_model), lambda b: (0, 0)),
            pl.BlockSpec((d_model, 2 * C_WIDTH), lambda b: (0, 0)),
        ],
        out_specs=[
            pl.BlockSpec((None, mem_len, C_WIDTH), lambda b: (b, 0, 0)),
            pl.BlockSpec((None, mem_len, C_WIDTH), lambda b: (b, 0, 0)),
        ],
        out_shape=[jax.ShapeDtypeStruct((batch, mem_len, C_WIDTH), _BF16)] * 2,
        compiler_params=params(dimension_semantics=("arbitrary",)),
        name="mem_kv",
    )(mem, mem_norm_g[0].reshape(1, d_model), w_mem_kv[0].astype(_BF16))

    tiles_per_seq = seq // PROJ_ROWS
    blocks_per_tile = PROJ_ROWS // MOBA_BLOCK
    row_spec = lambda w: pl.BlockSpec((PROJ_ROWS, w), lambda i: (i, 0))
    blk_t_spec = pl.BlockSpec((blocks_per_tile, A_WIDTH, MOBA_BLOCK), lambda i: (i, 0, 0))
    pos_spec = pl.BlockSpec((PROJ_ROWS, HEAD_DIM), lambda i: (i % tiles_per_seq, 0))
    mem_spec = pl.BlockSpec((None, mem_len, C_WIDTH), lambda i: (i // tiles_per_seq, 0, 0))
    blk_t_shape = jax.ShapeDtypeStruct((n_tok // MOBA_BLOCK, A_WIDTH, MOBA_BLOCK), _BF16)
    qt, ka, vt, sga, kmean, ybc, sma = pl.pallas_call(
        _proj_kernel,
        grid=(n_tok // PROJ_ROWS,),
        in_specs=[
            row_spec(d_model),
            _const_spec((1, d_model)),
            pl.BlockSpec(memory_space=pl.ANY),
            pos_spec, pos_spec,
            _const_spec((1, B_WIDTH)),
            _const_spec((GMLP_GROUPS, GMLP_CHUNK, GMLP_CHUNK)),
            _const_spec((GMLP_CHUNK, GMLP_GROUPS)),
            mem_spec, mem_spec,
            _const_spec((B_WIDTH, d_model)),
            _const_spec((C_WIDTH, d_model)),
        ],
        out_specs=[
            blk_t_spec, row_spec(A_WIDTH), blk_t_spec, row_spec(A_WIDTH),
            pl.BlockSpec((None, blocks_per_tile, A_WIDTH), lambda i: (i, 0, 0)),
            row_spec(d_model), row_spec(d_model),
        ],
        out_shape=[
            blk_t_shape,
            jax.ShapeDtypeStruct((n_tok, A_WIDTH), _BF16),
            blk_t_shape,
            jax.ShapeDtypeStruct((n_tok, A_WIDTH), _BF16),
            jax.ShapeDtypeStruct((n_tok // PROJ_ROWS, blocks_per_tile, A_WIDTH), _F32),
            jax.ShapeDtypeStruct((n_tok, d_model), _BF16),
            jax.ShapeDtypeStruct((n_tok, d_model), _BF16),
        ],
        scratch_shapes=[
            pltpu.VMEM((d_model, in_width), _BF16),
            pltpu.VMEM((2, d_model, W_IN_CHUNK), _F32),
            pltpu.SemaphoreType.DMA((2,)),
            pltpu.VMEM((PROJ_ROWS, B_WIDTH), _BF16),
            pltpu.VMEM((PROJ_ROWS, C_WIDTH), _BF16),
        ],
        compiler_params=params(dimension_semantics=("arbitrary",)),
        name="proj_branches",
    )(x2, norm_g[0].reshape(1, d_model), w_in[0], cos_t, sin_t,
      gmlp_ln_g[0].reshape(1, B_WIDTH), w_spatial[0], bsp_t, mk, mv,
      w_branch_b[0].astype(_BF16), w_branch_c[0].astype(_BF16))

    kmean = kmean.reshape(batch, n_blocks, A_WIDTH)
    k3 = ka.reshape(batch, seq, A_WIDTH)
    vt4 = vt.reshape(batch, n_blocks, A_WIDTH, MOBA_BLOCK)
    tile_spec = lambda w: pl.BlockSpec((MOBA_BLOCK, w), lambda b, i: (b * n_blocks + i, 0))
    out = pl.pallas_call(
        _moba_out_kernel,
        grid=(batch, n_blocks),
        in_specs=[
            pl.BlockSpec((None, A_WIDTH, MOBA_BLOCK), lambda b, i: (b * n_blocks + i, 0, 0)),
            pl.BlockSpec(memory_space=pl.ANY),
            pl.BlockSpec(memory_space=pl.ANY),
            pl.BlockSpec((None, n_blocks, A_WIDTH), lambda b, i: (b, 0, 0)),
            tile_spec(A_WIDTH), tile_spec(d_model), tile_spec(d_model), tile_spec(d_model),
            _const_spec((A_WIDTH, d_model)),
            _const_spec((d_model, d_model)),
            _const_spec((1, d_model)),
        ],
        out_specs=tile_spec(d_model),
        out_shape=jax.ShapeDtypeStruct((n_tok, d_model), _F32),
        scratch_shapes=[
            pltpu.VMEM((seq, A_WIDTH), _BF16),
            pltpu.VMEM((n_blocks, A_WIDTH, MOBA_BLOCK), _BF16),
            pltpu.SemaphoreType.DMA((2, 2)),
            pltpu.VMEM((MOBA_HEADS, AUG_DIM, MOBA_BLOCK), _BF16),
            pltpu.VMEM((MOBA_HEADS, MOBA_BLOCK, MOBA_BLOCK), _F32),
            pltpu.VMEM((MOBA_HEADS, MOBA_BLOCK, MOBA_BLOCK), _F32),
            pltpu.VMEM((MOBA_HEADS, MOBA_BLOCK, MOBA_BLOCK), _F32),
            pltpu.VMEM((MOBA_HEADS, 1, MOBA_BLOCK), _F32),
            pltpu.VMEM((MOBA_HEADS, 1, MOBA_BLOCK), _F32),
            pltpu.VMEM((MOBA_HEADS, HEAD_DIM, MOBA_BLOCK), _F32),
            pltpu.VMEM((MOBA_BLOCK, A_WIDTH), _BF16),
        ],
        compiler_params=params(dimension_semantics=("arbitrary", "arbitrary")),
        name="moba_merge_out",
    )(qt, k3, vt4, kmean, sga, ybc, sma, x2,
      w_branch_a[0].astype(_BF16), w_out[0].astype(_BF16), final_norm_g.reshape(1, d_model))
    return out.reshape(batch, seq, d_model)
```

```python
import functools
import math

import jax
import jax.numpy as jnp
import numpy as np
from jax import lax
from jax.experimental import pallas as pl
from jax.experimental.pallas import tpu as pltpu

HEAD_DIM = 128
MOBA_HEADS = 6
MOBA_BLOCK = 256
MOBA_TOPK = 3
GMLP_GROUPS = 6
GMLP_CHUNK = 128
XATTN_HEADS = 4
ROPE_THETA = 500000.0
ROT_DIM = HEAD_DIM // 4
EPS = 1e-6

A_WIDTH = MOBA_HEADS * HEAD_DIM
B_WIDTH = GMLP_GROUPS * HEAD_DIM
C_WIDTH = XATTN_HEADS * HEAD_DIM

V7X_VMEM_LIMIT_BYTES = 58 * 1024 * 1024

PROJ_ROWS = 512

QK_LOG2_SCALE = HEAD_DIM ** -0.5 * math.log2(math.e)
MASK_BIAS = -1e30
AUG_DIM = 2 * HEAD_DIM

_BF16 = jnp.bfloat16
_F32 = jnp.float32
_NEG_INF = float("-inf")


def _rms_norm(x, g):
    return x * lax.rsqrt(jnp.mean(x * x, axis=-1, keepdims=True) + EPS) * g


def _silu(x):
    return x * jax.nn.sigmoid(x)


def _dot(a, b):
    return jnp.dot(a, b, preferred_element_type=_F32)


def _dot_nt(a, b):
    return lax.dot_general(a, b, (((1,), (1,)), ((), ())), preferred_element_type=_F32)


def _mem_kv_kernel(mem_ref, g_ref, w_ref, mk_ref, mv_ref):
    mn = _rms_norm(mem_ref[...], g_ref[...]).astype(_BF16)
    kv = _dot(mn, w_ref[...])
    mk_ref[...] = kv[:, :C_WIDTH].astype(_BF16)
    mv_ref[...] = kv[:, C_WIDTH:].astype(_BF16)


def _proj_kernel(x_ref, g_ref, w_ref, cos_ref, sin_ref, lng_ref, wsp_ref, bsp_ref,
                 mk_ref, mv_ref, wb_ref, wc_ref,
                 qt_ref, k_ref, vt_ref, sga_ref, kmean_ref, ybc_ref, sma_ref,
                 outb_scr, outc_scr):
    rows = x_ref.shape[0]
    d_model = x_ref.shape[1]
    h = _rms_norm(x_ref[...], g_ref[...]).astype(_BF16)

    off = [0]

    def proj(width):
        a = off[0]
        off[0] = a + width
        return _dot(h, w_ref[:, a:a + width])

    cos = cos_ref[...]
    sin = sin_ref[...]
    lane = lax.broadcasted_iota(jnp.int32, (rows, HEAD_DIM), 1)
    first_half = lane < (ROT_DIM // 2)

    def rope(t):
        swapped = jnp.where(first_half,
                            pltpu.roll(t, HEAD_DIM - ROT_DIM // 2, 1),
                            pltpu.roll(t, ROT_DIM // 2, 1))
        return t * cos + swapped * sin

    qa = proj(A_WIDTH)
    for hd in range(MOBA_HEADS):
        sl = slice(hd * HEAD_DIM, (hd + 1) * HEAD_DIM)
        qh = rope(qa[:, sl]) * QK_LOG2_SCALE
        for r in range(rows // MOBA_BLOCK):
            rs = slice(r * MOBA_BLOCK, (r + 1) * MOBA_BLOCK)
            qt_ref[r, sl, :] = qh[rs].T.astype(_BF16)
    ka = proj(A_WIDTH)
    for hd in range(MOBA_HEADS):
        sl = slice(hd * HEAD_DIM, (hd + 1) * HEAD_DIM)
        kh = rope(ka[:, sl])
        k_ref[:, sl] = kh.astype(_BF16)
        kmean_ref[:, sl] = jnp.mean(
            kh.reshape(rows // MOBA_BLOCK, MOBA_BLOCK, HEAD_DIM), axis=1)
    va = proj(A_WIDTH)
    for r in range(rows // MOBA_BLOCK):
        rs = slice(r * MOBA_BLOCK, (r + 1) * MOBA_BLOCK)
        vt_ref[r] = va[rs].T.astype(_BF16)
    sga_ref[...] = _silu(proj(A_WIDTH)).astype(_BF16)

    ub = proj(B_WIDTH)
    vb = proj(B_WIDTH)
    gb = proj(B_WIDTH)
    mu = jnp.mean(vb, axis=-1, keepdims=True)
    var = jnp.mean(jnp.square(vb - mu), axis=-1, keepdims=True)
    vn = ((vb - mu) * lax.rsqrt(var + EPS) * lng_ref[...]).astype(_BF16)
    gate_b = ub * _silu(gb)
    r_i = lax.broadcasted_iota(jnp.int32, (GMLP_CHUNK, GMLP_CHUNK), 0)
    c_i = lax.broadcasted_iota(jnp.int32, (GMLP_CHUNK, GMLP_CHUNK), 1)
    causal = r_i >= c_i
    bsp = bsp_ref[...]
    chunks = [slice(c * GMLP_CHUNK, (c + 1) * GMLP_CHUNK) for c in range(rows // GMLP_CHUNK)]
    for g in range(GMLP_GROUPS):
        sl = slice(g * HEAD_DIM, (g + 1) * HEAD_DIM)
        wg = jnp.where(causal, wsp_ref[g], 0.0).astype(_BF16)
        mixed = _dot(wg, jnp.concatenate([vn[rs, sl] for rs in chunks], axis=1))
        bias = bsp[:, g:g + 1]
        for c, rs in enumerate(chunks):
            mixed_c = mixed[:, c * HEAD_DIM:(c + 1) * HEAD_DIM] + bias
            outb_scr[rs, sl] = (gate_b[rs, sl] * mixed_c).astype(_BF16)

    qc = proj(C_WIDTH).astype(_BF16)
    gc = proj(C_WIDTH)
    scale = HEAD_DIM ** -0.5
    for hd in range(XATTN_HEADS):
        sl = slice(hd * HEAD_DIM, (hd + 1) * HEAD_DIM)
        s = _dot_nt(qc[:, sl], mk_ref[:, sl]) * scale
        m = jnp.max(s, axis=-1, keepdims=True)
        p = jnp.exp(s - m)
        l = jnp.sum(p, axis=-1, keepdims=True)
        o = _dot(p.astype(_BF16), mv_ref[:, sl]) / l
        outc_scr[:, sl] = (o * _silu(gc[:, sl])).astype(_BF16)

    sma_ref[...] = jax.nn.sigmoid(proj(d_model)).astype(_BF16)
    yb = jax.nn.sigmoid(proj(d_model)) * _dot(outb_scr[...], wb_ref[...])
    yc = jax.nn.sigmoid(proj(d_model)) * _dot(outc_scr[...], wc_ref[...])
    ybc_ref[...] = (yb + yc).astype(_BF16)


def _moba_out_kernel(qt_ref, k_hbm, vt_hbm, kmean_ref, sga_ref, ybc_ref, sma_ref, x_ref,
                     wa_ref, wo_ref, fg_ref, o_ref,
                     k_ref, vt_ref, kv_sem,
                     qaug_scr, sa_scr, sb_scr, sd_scr, m_scr, l_scr, acc_scr, outa_scr):
    batch_i = pl.program_id(0)
    own = pl.program_id(1)
    n_batch = pl.num_programs(0)
    n_blocks = kmean_ref.shape[0]
    tq = qt_ref.shape[1]

    def kv_copies(b, j):
        rows = pl.ds(pl.multiple_of(j * MOBA_BLOCK, MOBA_BLOCK), MOBA_BLOCK)
        return (pltpu.make_async_copy(k_hbm.at[b, rows, :], k_ref.at[rows, :],
                                      kv_sem.at[0, j % 2]),
                pltpu.make_async_copy(vt_hbm.at[b, j], vt_ref.at[j], kv_sem.at[1, j % 2]))

    def kv_start(b, j):
        for copy in kv_copies(b, j):
            copy.start()

    def kv_wait(b, j):
        for copy in kv_copies(b, j):
            copy.wait()

    @pl.when(jnp.logical_and(batch_i == 0, own == 0))
    def _():
        kv_start(0, 0)
        kv_start(0, 1)

    @pl.when(own == 0)
    def _():
        kv_wait(batch_i, 0)
        kv_wait(batch_i, 1)

    @pl.when(jnp.logical_and(own >= 1, own + 1 < n_blocks))
    def _():
        kv_wait(batch_i, own + 1)

    @pl.when(own + 2 < n_blocks)
    def _():
        kv_start(batch_i, own + 2)

    blk = lax.broadcasted_iota(jnp.int32, (n_blocks, tq), 0)
    valid = blk < own
    key_i = lax.broadcasted_iota(jnp.int32, (MOBA_BLOCK, HEAD_DIM), 0)
    qry_i = lax.broadcasted_iota(jnp.int32, (MOBA_BLOCK, HEAD_DIM), 1)
    own_start = pl.multiple_of(own * MOBA_BLOCK, MOBA_BLOCK)

    for hd in range(MOBA_HEADS):
        sl = slice(hd * HEAD_DIM, (hd + 1) * HEAD_DIM)
        qt = qt_ref[sl, :]

        gs = _dot(kmean_ref[:, sl].astype(_BF16), qt)
        gs = jnp.where(valid, gs, _NEG_INF)
        sel = jnp.zeros((n_blocks, tq), dtype=jnp.bool_)
        for _ in range(MOBA_TOPK):
            top = jnp.max(gs, axis=0, keepdims=True)
            idx = jnp.min(jnp.where(gs == top, blk, n_blocks), axis=0, keepdims=True)
            pick = blk == idx
            sel = jnp.logical_or(sel, pick)
            gs = jnp.where(pick, _NEG_INF, gs)
        bias = jnp.where(jnp.logical_and(sel, valid), 0.0, MASK_BIAS)
        qaug_scr[hd, 0:HEAD_DIM, :] = qt
        qaug_scr[hd, HEAD_DIM:HEAD_DIM + n_blocks, :] = bias.astype(_BF16)
        qaug_scr[hd, HEAD_DIM + n_blocks:AUG_DIM, :] = jnp.zeros(
            (AUG_DIM - HEAD_DIM - n_blocks, tq), _BF16)

        sd_scr[hd] = _dot(k_ref[pl.ds(own_start, MOBA_BLOCK), sl], qt)
        m_scr[hd] = jnp.full((1, tq), _NEG_INF, _F32)
        l_scr[hd] = jnp.zeros((1, tq), _F32)
        acc_scr[hd] = jnp.zeros((HEAD_DIM, tq), _F32)

    lane = lax.broadcasted_iota(jnp.int32, (MOBA_BLOCK, HEAD_DIM), 1)

    def masked_scores(j, hd):
        sl = slice(hd * HEAD_DIM, (hd + 1) * HEAD_DIM)
        start = pl.multiple_of(j * MOBA_BLOCK, MOBA_BLOCK)
        onehot = jnp.where(lane == j, 1.0, 0.0).astype(_BF16)
        kaug = jnp.concatenate([k_ref[pl.ds(start, MOBA_BLOCK), sl], onehot], axis=1)
        return _dot(kaug, qaug_scr[hd])

    def softmax_pv(j, hd, s_ref, causal=False):
        sl = slice(hd * HEAD_DIM, (hd + 1) * HEAD_DIM)
        ps, alphas = [], []
        for half in range(tq // HEAD_DIM):
            cs = slice(half * HEAD_DIM, (half + 1) * HEAD_DIM)
            s = s_ref[hd, :, cs]
            if causal:
                s = jnp.where(key_i <= qry_i + half * HEAD_DIM, s, _NEG_INF)
            m_old = m_scr[hd, :, cs]
            m_new = jnp.maximum(m_old, jnp.max(s, axis=0, keepdims=True))
            alpha = jnp.exp2(m_old - m_new)
            p = jnp.exp2(s - m_new)
            m_scr[hd, :, cs] = m_new
            l_scr[hd, :, cs] = alpha * l_scr[hd, :, cs] + jnp.sum(p, axis=0, keepdims=True)
            ps.append(p.astype(_BF16))
            alphas.append(alpha)
        pv = _dot(vt_ref[j, sl, :], jnp.concatenate(ps, axis=1))
        for half, alpha in enumerate(alphas):
            cs = slice(half * HEAD_DIM, (half + 1) * HEAD_DIM)
            acc_scr[hd, :, cs] = alpha * acc_scr[hd, :, cs] + pv[:, cs]

    for hd in range(MOBA_HEADS):
        sa_scr[hd] = masked_scores(0, hd)
        sb_scr[hd] = masked_scores(1, hd)

    def pair_trip(jj, score_even, score_odd):
        j0 = 2 * jj
        for hd in range(MOBA_HEADS):
            s_next = masked_scores(j0 + 2, hd) if score_even else None
            softmax_pv(j0, hd, sa_scr)
            if score_even:
                sa_scr[hd] = s_next
        for hd in range(MOBA_HEADS):
            s_next = masked_scores(j0 + 3, hd) if score_odd else None
            softmax_pv(j0 + 1, hd, sb_scr)
            if score_odd:
                sb_scr[hd] = s_next

    n_pairs = own // 2

    def pair_body(jj, carry):
        pair_trip(jj, True, True)
        return carry

    lax.fori_loop(0, n_pairs - 1, pair_body, 0)

    @pl.when(jnp.logical_and(n_pairs >= 1, own % 2 == 1))
    def _():
        pair_trip(n_pairs - 1, True, False)
        for hd in range(MOBA_HEADS):
            softmax_pv(own - 1, hd, sa_scr)

    @pl.when(jnp.logical_and(n_pairs >= 1, own % 2 == 0))
    def _():
        pair_trip(n_pairs - 1, False, False)

    @pl.when(own == 1)
    def _():
        for hd in range(MOBA_HEADS):
            softmax_pv(0, hd, sa_scr)

    for hd in range(MOBA_HEADS):
        softmax_pv(own, hd, sd_scr, causal=True)

    for hd in range(MOBA_HEADS):
        sl = slice(hd * HEAD_DIM, (hd + 1) * HEAD_DIM)
        o = (acc_scr[hd] / l_scr[hd]).T
        outa_scr[:, sl] = (o * sga_ref[:, sl].astype(_F32)).astype(_BF16)

    ya = _dot(outa_scr[...], wa_ref[...])
    y = sma_ref[...].astype(_F32) * ya + ybc_ref[...].astype(_F32)
    z = x_ref[...] + _dot(y.astype(_BF16), wo_ref[...])
    o_ref[...] = _rms_norm(z, fg_ref[...])

    @pl.when(jnp.logical_and(own == n_blocks - 1, batch_i + 1 < n_batch))
    def _():
        kv_start(batch_i + 1, 0)
        kv_start(batch_i + 1, 1)


def _rope_tables(seq):
    inv_freq = np.float32(ROPE_THETA) ** (-np.arange(0, ROT_DIM, 2, dtype=np.float32) / ROT_DIM)
    ang = np.arange(seq).astype(np.float32)[:, None] * inv_freq[None, :].astype(np.float32)
    cos, sin = np.cos(ang), np.sin(ang)
    rest = HEAD_DIM - ROT_DIM
    cos_t = np.concatenate([cos, cos, np.ones((seq, rest), np.float32)], axis=1)
    sin_t = np.concatenate([-sin, sin, np.zeros((seq, rest), np.float32)], axis=1)
    return jnp.asarray(cos_t, _F32), jnp.asarray(sin_t, _F32)


def _const_spec(shape):
    nd = len(shape)
    return pl.BlockSpec(shape, lambda *_: (0,) * nd, pipeline_mode=pl.Buffered(1))


def kernel(x, mem, norm_g, mem_norm_g, final_norm_g, w_in, w_mem_kv, gmlp_ln_g,
           w_spatial, b_spatial, w_branch_a, w_branch_b, w_branch_c, w_out):
    batch, seq, d_model = x.shape
    mem_len = mem.shape[1]
    assert w_in.shape[0] == 1, "single layer"
    assert seq % PROJ_ROWS == 0 and PROJ_ROWS % MOBA_BLOCK == 0 and PROJ_ROWS % GMLP_CHUNK == 0
    assert w_spatial.shape[1:] == (GMLP_GROUPS, GMLP_CHUNK, GMLP_CHUNK)
    n_tok = batch * seq
    n_blocks = seq // MOBA_BLOCK
    assert HEAD_DIM + n_blocks <= AUG_DIM and n_blocks % 16 == 0
    in_width = w_in.shape[-1]

    x2 = x.reshape(n_tok, d_model)
    w_in_b = w_in[0].astype(_BF16)
    cos_t, sin_t = _rope_tables(seq)
    bsp_t = jnp.transpose(b_spatial[0])

    params = functools.partial(pltpu.CompilerParams, vmem_limit_bytes=V7X_VMEM_LIMIT_BYTES)

    mk, mv = pl.pallas_call(
        _mem_kv_kernel,
        grid=(batch,),
        in_specs=[
            pl.BlockSpec((None, mem_len, d_model), lambda b: (b, 0, 0)),
            pl.BlockSpec((1, d_model), lambda b: (0, 0)),
            pl.BlockSpec((d_model, 2 * C_WIDTH), lambda b: (0, 0)),
        ],
        out_specs=[
            pl.BlockSpec((None, mem_len, C_WIDTH), lambda b: (b, 0, 0)),
            pl.BlockSpec((None, mem_len, C_WIDTH), lambda b: (b, 0, 0)),
        ],
        out_shape=[jax.ShapeDtypeStruct((batch, mem_len, C_WIDTH), _BF16)] * 2,
        compiler_params=params(dimension_semantics=("arbitrary",)),
        name="mem_kv",
    )(mem, mem_norm_g[0].reshape(1, d_model), w_mem_kv[0].astype(_BF16))

    tiles_per_seq = seq // PROJ_ROWS
    blocks_per_tile = PROJ_ROWS // MOBA_BLOCK
    row_spec = lambda w: pl.BlockSpec((PROJ_ROWS, w), lambda i: (i, 0))
    blk_t_spec = pl.BlockSpec((blocks_per_tile, A_WIDTH, MOBA_BLOCK), lambda i: (i, 0, 0))
    pos_spec = pl.BlockSpec((PROJ_ROWS, HEAD_DIM), lambda i: (i % tiles_per_seq, 0))
    mem_spec = pl.BlockSpec((None, mem_len, C_WIDTH), lambda i: (i // tiles_per_seq, 0, 0))
    blk_t_shape = jax.ShapeDtypeStruct((n_tok // MOBA_BLOCK, A_WIDTH, MOBA_BLOCK), _BF16)
    qt, ka, vt, sga, kmean, ybc, sma = pl.pallas_call(
        _proj_kernel,
        grid=(n_tok // PROJ_ROWS,),
        in_specs=[
            row_spec(d_model),
            _const_spec((1, d_model)),
            _const_spec((d_model, in_width)),
            pos_spec, pos_spec,
            _const_spec((1, B_WIDTH)),
            _const_spec((GMLP_GROUPS, GMLP_CHUNK, GMLP_CHUNK)),
            _const_spec((GMLP_CHUNK, GMLP_GROUPS)),
            mem_spec, mem_spec,
            _const_spec((B_WIDTH, d_model)),
            _const_spec((C_WIDTH, d_model)),
        ],
        out_specs=[
            blk_t_spec, row_spec(A_WIDTH), blk_t_spec, row_spec(A_WIDTH),
            pl.BlockSpec((None, blocks_per_tile, A_WIDTH), lambda i: (i, 0, 0)),
            row_spec(d_model), row_spec(d_model),
        ],
        out_shape=[
            blk_t_shape,
            jax.ShapeDtypeStruct((n_tok, A_WIDTH), _BF16),
            blk_t_shape,
            jax.ShapeDtypeStruct((n_tok, A_WIDTH), _BF16),
            jax.ShapeDtypeStruct((n_tok // PROJ_ROWS, blocks_per_tile, A_WIDTH), _F32),
            jax.ShapeDtypeStruct((n_tok, d_model), _BF16),
            jax.ShapeDtypeStruct((n_tok, d_model), _BF16),
        ],
        scratch_shapes=[
            pltpu.VMEM((PROJ_ROWS, B_WIDTH), _BF16),
            pltpu.VMEM((PROJ_ROWS, C_WIDTH), _BF16),
        ],
        compiler_params=params(dimension_semantics=("arbitrary",)),
        name="proj_branches",
    )(x2, norm_g[0].reshape(1, d_model), w_in_b, cos_t, sin_t,
      gmlp_ln_g[0].reshape(1, B_WIDTH), w_spatial[0], bsp_t, mk, mv,
      w_branch_b[0].astype(_BF16), w_branch_c[0].astype(_BF16))

    kmean = kmean.reshape(batch, n_blocks, A_WIDTH)
    k3 = ka.reshape(batch, seq, A_WIDTH)
    vt4 = vt.reshape(batch, n_blocks, A_WIDTH, MOBA_BLOCK)
    tile_spec = lambda w: pl.BlockSpec((MOBA_BLOCK, w), lambda b, i: (b * n_blocks + i, 0))
    out = pl.pallas_call(
        _moba_out_kernel,
        grid=(batch, n_blocks),
        in_specs=[
            pl.BlockSpec((None, A_WIDTH, MOBA_BLOCK), lambda b, i: (b * n_blocks + i, 0, 0)),
            pl.BlockSpec(memory_space=pl.ANY),
            pl.BlockSpec(memory_space=pl.ANY),
            pl.BlockSpec((None, n_blocks, A_WIDTH), lambda b, i: (b, 0, 0)),
            tile_spec(A_WIDTH), tile_spec(d_model), tile_spec(d_model), tile_spec(d_model),
            _const_spec((A_WIDTH, d_model)),
            _const_spec((d_model, d_model)),
            _const_spec((1, d_model)),
        ],
        out_specs=tile_spec(d_model),
        out_shape=jax.ShapeDtypeStruct((n_tok, d_model), _F32),
        scratch_shapes=[
            pltpu.VMEM((seq, A_WIDTH), _BF16),
            pltpu.VMEM((n_blocks, A_WIDTH, MOBA_BLOCK), _BF16),
            pltpu.SemaphoreType.DMA((2, 2)),
            pltpu.VMEM((MOBA_HEADS, AUG_DIM, MOBA_BLOCK), _BF16),
            pltpu.VMEM((MOBA_HEADS, MOBA_BLOCK, MOBA_BLOCK), _F32),
            pltpu.VMEM((MOBA_HEADS, MOBA_BLOCK, MOBA_BLOCK), _F32),
            pltpu.VMEM((MOBA_HEADS, MOBA_BLOCK, MOBA_BLOCK), _F32),
            pltpu.VMEM((MOBA_HEADS, 1, MOBA_BLOCK), _F32),
            pltpu.VMEM((MOBA_HEADS, 1, MOBA_BLOCK), _F32),
            pltpu.VMEM((MOBA_HEADS, HEAD_DIM, MOBA_BLOCK), _F32),
            pltpu.VMEM((MOBA_BLOCK, A_WIDTH), _BF16),
        ],
        compiler_params=params(dimension_semantics=("arbitrary", "arbitrary")),
        name="moba_merge_out",
    )(qt, k3, vt4, kmean, sga, ybc, sma, x2,
      w_branch_a[0].astype(_BF16), w_out[0].astype(_BF16), final_norm_g.reshape(1, d_model))
    return out.reshape(batch, seq, d_model)
```
